```python
import jax, jax.numpy as jnp
from jax import lax
import numpy as np

D_MODEL = 1024
BATCH = 8
SEQ = 2048
DEPTH = 4

HEAD_DIM = 64
N_HEADS = D_MODEL // HEAD_DIM
ATTN_DIM = N_HEADS * HEAD_DIM
D_FF = 256 * ((8 * D_MODEL // 3 + 255) // 256)
ROPE_THETA = 10000.0
RMS_EPS = 1e-6
N_A_LAYERS = DEPTH // 2
N_B_LAYERS = DEPTH - N_A_LAYERS
MOBA_BLOCK = 256
MOBA_TOPK = 3
MOBA_Q_CHUNK = 16
DILATED_BRANCHES = ((128, 1), (512, 4), (2048, 16))
N_BRANCHES = len(DILATED_BRANCHES)
N_SUBLAYERS = 3

kernel_name = "yoco_moba_longnet_macaron_adaln"


def rms_norm(x, g):
    x32 = x.astype(jnp.float32)
    y = x32 * lax.rsqrt(jnp.mean(x32 * x32, axis=-1, keepdims=True) + RMS_EPS)
    return (y * g.astype(jnp.float32)).astype(x.dtype)


def modulate(h, shift, scale):
    return h * (1 + scale[:, None, :]) + shift[:, None, :]


def rope_tables(seq_len):
    inv = 1.0 / (ROPE_THETA ** (jnp.arange(0, HEAD_DIM, 2, dtype=jnp.float32) / HEAD_DIM))
    ang = jnp.arange(seq_len, dtype=jnp.float32)[:, None] * inv[None, :]
    return jnp.cos(ang), jnp.sin(ang)


def apply_rope(x, cos, sin):
    x32 = x.astype(jnp.float32)
    x1, x2 = jnp.split(x32, 2, axis=-1)
    c = cos[None, :, None, :]
    s = sin[None, :, None, :]
    return jnp.concatenate([x1 * c - x2 * s, x2 * c + x1 * s], axis=-1).astype(x.dtype)


def swiglu(h, w_gate, w_up, w_down):
    return (jax.nn.silu(h @ w_gate) * (h @ w_up)) @ w_down


def moba_attention(q, k, v):
    B, S, H, Dh = q.shape
    nb = -(-S // MOBA_BLOCK)
    Sp = nb * MOBA_BLOCK
    pad = ((0, 0), (0, Sp - S), (0, 0), (0, 0))
    q, k, v = jnp.pad(q, pad), jnp.pad(k, pad), jnp.pad(v, pad)
    scale = Dh ** -0.5
    kb = k.reshape(B, nb, MOBA_BLOCK, H, Dh)
    vb = v.reshape(B, nb, MOBA_BLOCK, H, Dh)
    k_mean = jnp.mean(kb.astype(jnp.float32), axis=2)
    gate = jnp.einsum('bshd,bnhd->bshn', q.astype(jnp.float32), k_mean)
    q_block = jnp.arange(Sp) // MOBA_BLOCK
    past = jnp.arange(nb)[None, :] < q_block[:, None]
    gate = jnp.where(past[None, :, None, :], gate, -jnp.inf)
    n_sel = min(MOBA_TOPK, max(nb - 1, 1))
    top_val, top_idx = lax.top_k(gate, n_sel)
    sel_valid = top_val > -jnp.inf
    kb_t = kb.transpose(0, 3, 1, 2, 4)
    vb_t = vb.transpose(0, 3, 1, 2, 4)
    n_chunks = Sp // MOBA_Q_CHUNK

    def to_chunks(t):
        return t.reshape(B, n_chunks, MOBA_Q_CHUNK, *t.shape[2:]).swapaxes(0, 1)

    b_ix = jnp.arange(B)[:, None, None, None]
    h_ix = jnp.arange(H)[None, None, :, None]
    in_blk = jnp.arange(MOBA_BLOCK)

    def chunk_fn(args):
        start, qc, idx, valid = args
        blk = start // MOBA_BLOCK
        k_own = lax.dynamic_slice_in_dim(k, blk * MOBA_BLOCK, MOBA_BLOCK, axis=1)
        v_own = lax.dynamic_slice_in_dim(v, blk * MOBA_BLOCK, MOBA_BLOCK, axis=1)
        q_pos = start + jnp.arange(MOBA_Q_CHUNK)
        k_pos = blk * MOBA_BLOCK + in_blk
        s_own = jnp.einsum('bqhd,bkhd->bqhk', qc, k_own).astype(jnp.float32) * scale
        causal = (k_pos[None, :] <= q_pos[:, None])[None, :, None, :]
        s_own = jnp.where(causal, s_own, -jnp.inf)
        k_sel = kb_t[b_ix, h_ix, idx]
        v_sel = vb_t[b_ix, h_ix, idx]
        s_sel = jnp.einsum('bqhd,bqhnkd->bqhnk', qc, k_sel).astype(jnp.float32) * scale
        s_sel = jnp.where(valid[..., None], s_sel, -jnp.inf)
        s_sel = s_sel.reshape(B, MOBA_Q_CHUNK, H, n_sel * MOBA_BLOCK)
        p = jax.nn.softmax(jnp.concatenate([s_own, s_sel], axis=-1), axis=-1).astype(v.dtype)
        p_own = p[..., :MOBA_BLOCK]
        p_sel = p[..., MOBA_BLOCK:].reshape(B, MOBA_Q_CHUNK, H, n_sel, MOBA_BLOCK)
        return (jnp.einsum('bqhk,bkhd->bqhd', p_own, v_own)
                + jnp.einsum('bqhnk,bqhnkd->bqhd', p_sel, v_sel))

    starts = jnp.arange(n_chunks, dtype=jnp.int32) * MOBA_Q_CHUNK
    out = lax.map(chunk_fn, (starts, to_chunks(q), to_chunks(top_idx), to_chunks(sel_valid)))
    out = out.swapaxes(0, 1).reshape(B, Sp, H, Dh)
    return out[:, :S]


def moba_mixer(h, w_qkv, w_o, cos, sin):
    B, S, _ = h.shape
    qkv = (h @ w_qkv).reshape(B, S, 3, N_HEADS, HEAD_DIM)
    q = apply_rope(qkv[:, :, 0], cos, sin)
    k = apply_rope(qkv[:, :, 1], cos, sin)
    v = qkv[:, :, 2]
    return moba_attention(q, k, v).reshape(B, S, ATTN_DIM) @ w_o


def to_strided(x, d):
    B, S = x.shape[:2]
    return x.reshape(B, S // d, d, *x.shape[2:]).swapaxes(1, 2).reshape(B * d, S // d, *x.shape[2:])


def from_strided(x, d, B):
    L = x.shape[1]
    return x.reshape(B, d, L, *x.shape[2:]).swapaxes(1, 2).reshape(B, L * d, *x.shape[2:])


def band_kv(t, n_dist):
    N, L = t.shape[:2]
    nb = -(-L // n_dist)
    tp = jnp.pad(t, ((0, 0), (n_dist, nb * n_dist - L), (0, 0), (0, 0)))
    tb = tp.reshape(N, nb + 1, n_dist, *t.shape[2:])
    return jnp.concatenate([tb[:, :-1], tb[:, 1:]], axis=2)


def band_attention(q, k_band, v_band, n_dist):
    N, L, H, Dh = q.shape
    nb = k_band.shape[1]
    qb = jnp.pad(q, ((0, 0), (0, nb * n_dist - L), (0, 0), (0, 0))).reshape(N, nb, n_dist, H, Dh)
    s = jnp.einsum('nbqhd,nbkhd->nbhqk', qb, k_band).astype(jnp.float32) * (Dh ** -0.5)
    a = jnp.arange(n_dist)[:, None]
    j = jnp.arange(2 * n_dist)[None, :]
    blk = jnp.arange(nb)[:, None, None]
    allowed = (j >= a) & (j <= a + n_dist) & (blk * n_dist + j >= n_dist)
    s = jnp.where(allowed[None, :, None], s, -jnp.inf)
    m = jnp.max(s, axis=-1, keepdims=True)
    p = jnp.exp(s - m)
    l = jnp.sum(p, axis=-1, keepdims=True)
    o = jnp.einsum('nbhqk,nbkhd->nbqhd', (p / l).astype(v_band.dtype), v_band)
    lse = (m + jnp.log(l))[..., 0]
    o = o.reshape(N, nb * n_dist, H, Dh)[:, :L]
    lse = lse.transpose(0, 1, 3, 2).reshape(N, nb * n_dist, H)[:, :L]
    return o, lse


def shared_kv_bands(h, w_kv, cos, sin):
    B, S, _ = h.shape
    kv = (h @ w_kv).reshape(B, S, N_BRANCHES, 2, N_HEADS, HEAD_DIM)
    bands = []
    for g, (window, dil) in enumerate(DILATED_BRANCHES):
        n_dist = window // dil
        k = apply_rope(kv[:, :, g, 0], cos, sin)
        v = kv[:, :, g, 1]
        bands.append((band_kv(to_strided(k, dil), n_dist), band_kv(to_strided(v, dil), n_dist)))
    return bands


def dilated_mixer(h, w_q, w_o, kv_bands, cos, sin):
    B, S, _ = h.shape
    q = apply_rope((h @ w_q).reshape(B, S, N_BRANCHES * N_HEADS, HEAD_DIM), cos, sin)
    q = q.reshape(B, S, N_BRANCHES, N_HEADS, HEAD_DIM)
    outs, lses = [], []
    for g, (window, dil) in enumerate(DILATED_BRANCHES):
        k_band, v_band = kv_bands[g]
        o, lse = band_attention(to_strided(q[:, :, g], dil), k_band, v_band, window // dil)
        outs.append(from_strided(o, dil, B))
        lses.append(from_strided(lse, dil, B))
    w = jax.nn.softmax(jnp.stack(lses, axis=0), axis=0).astype(h.dtype)
    o = jnp.einsum('gbsh,gbshd->bshd', w, jnp.stack(outs, axis=0))
    return o.reshape(B, S, ATTN_DIM) @ w_o


def setup_inputs(seed: int = 0) -> dict:
    key = jax.random.key(seed)
    ks = jax.random.split(key, 17)
    D, F = D_MODEL, D_FF

    def w(k, shape, fan_in, gain=1.0):
        return jax.random.normal(k, shape, jnp.float32) * (gain * fan_in ** -0.5)

    def norm_gain(k, shape):
        return 1.0 + 0.05 * jax.random.normal(k, shape, jnp.float32)

    return {
        "x": jax.random.normal(ks[0], (BATCH, SEQ, D), jnp.float32),
        "c": jax.random.normal(ks[1], (BATCH, D), jnp.float32),
        "ada_w": w(ks[2], (DEPTH, D, N_SUBLAYERS * 3 * D), D, 0.5),
        "ada_b": 0.02 * jax.random.normal(ks[3], (DEPTH, N_SUBLAYERS * 3 * D), jnp.float32),
        "norm_g": norm_gain(ks[4], (DEPTH, N_SUBLAYERS, D)),
        "ffn_w_gate": w(ks[5], (DEPTH, 2, D, F), D),
        "ffn_w_up": w(ks[6], (DEPTH, 2, D, F), D),
        "ffn_w_down": w(ks[7], (DEPTH, 2, F, D), F),
        "moba_w_qkv": w(ks[8], (N_A_LAYERS, D, 3 * ATTN_DIM), D),
        "moba_w_o": w(ks[9], (N_A_LAYERS, ATTN_DIM, D), ATTN_DIM),
        "kv_ada_w": w(ks[10], (D, 2 * D), D, 0.5),
        "kv_ada_b": 0.02 * jax.random.normal(ks[11], (2 * D,), jnp.float32),
        "kv_norm_g": norm_gain(ks[12], (D,)),
        "kv_w": w(ks[13], (D, N_BRANCHES * 2 * ATTN_DIM), D),
        "dil_w_q": w(ks[14], (N_B_LAYERS, D, N_BRANCHES * ATTN_DIM), D),
        "dil_w_o": w(ks[15], (N_B_LAYERS, ATTN_DIM, D), ATTN_DIM),
        "final_g": norm_gain(ks[16], (D,)),
    }


def reference(x, c, ada_w, ada_b, norm_g, ffn_w_gate, ffn_w_up, ffn_w_down,
              moba_w_qkv, moba_w_o, kv_ada_w, kv_ada_b, kv_norm_g, kv_w,
              dil_w_q, dil_w_o, final_g):
    B, S, D = x.shape
    cos, sin = rope_tables(S)
    c_act = jax.nn.silu(c)
    kv_bands = None
    for layer in range(DEPTH):
        if layer == N_A_LAYERS:
            kv_mod = (c_act @ kv_ada_w + kv_ada_b).reshape(B, 2, D)
            h_kv = modulate(rms_norm(x, kv_norm_g), kv_mod[:, 0], kv_mod[:, 1])
            kv_bands = shared_kv_bands(h_kv, kv_w, cos, sin)
        mod = (c_act @ ada_w[layer] + ada_b[layer]).reshape(B, N_SUBLAYERS, 3, D)
        h = modulate(rms_norm(x, norm_g[layer, 0]), mod[:, 0, 0], mod[:, 0, 1])
        x = x + 0.5 * mod[:, 0, 2][:, None, :] * swiglu(h, ffn_w_gate[layer, 0], ffn_w_up[layer, 0], ffn_w_down[layer, 0])
        h = modulate(rms_norm(x, norm_g[layer, 1]), mod[:, 1, 0], mod[:, 1, 1])
        if layer < N_A_LAYERS:
            y = moba_mixer(h, moba_w_qkv[layer], moba_w_o[layer], cos, sin)
        else:
            lb = layer - N_A_LAYERS
            y = dilated_mixer(h, dil_w_q[lb], dil_w_o[lb], kv_bands, cos, sin)
        x = x + mod[:, 1, 2][:, None, :] * y
        h = modulate(rms_norm(x, norm_g[layer, 2]), mod[:, 2, 0], mod[:, 2, 1])
        x = x + 0.5 * mod[:, 2, 2][:, None, :] * swiglu(h, ffn_w_gate[layer, 1], ffn_w_up[layer, 1], ffn_w_down[layer, 1])
    return rms_norm(x, final_g)
```

```python
import functools

import jax
import jax.numpy as jnp
from jax import lax
from jax.experimental import pallas as pl
from jax.experimental.pallas import tpu as pltpu

F32 = jnp.float32
BF16 = jnp.bfloat16

D_MODEL = 1024
HEAD_DIM = 64
N_HEADS = D_MODEL // HEAD_DIM
N_HEAD_PAIRS = N_HEADS // 2
D_FF = 2816
ROPE_THETA = 10000.0
RMS_EPS = 1e-6
DEPTH = 4
N_A_LAYERS = 2
MOBA_BLOCK = 256
MOBA_TOPK = 3
DILATED_BRANCHES = ((128, 1), (512, 4), (2048, 16))
N_BRANCHES = len(DILATED_BRANCHES)
BAND = 128
LANES = 128
FFN_CHUNK = 256
TOKEN_TILE = 512
NEG_INF = float("-inf")
MIB = 1024 * 1024


def _dot(a, b):
    return jnp.dot(a, b, preferred_element_type=F32)


def _split_bf16(a):
    hi = a.astype(BF16)
    lo = (a - hi.astype(F32)).astype(BF16)
    return hi, lo


def _resident(shape):
    return pl.BlockSpec(shape, lambda *_: (0,) * len(shape),
                        pipeline_mode=pl.Buffered(1))


def _params(semantics, vmem_mib):
    return pltpu.CompilerParams(dimension_semantics=semantics,
                                vmem_limit_bytes=vmem_mib * MIB)


def _mod_kernel(c_ref, w_ref, b_ref, o_ref):
    c = c_ref[...]
    a_hi, a_lo = _split_bf16(c * jax.nn.sigmoid(c))
    w_hi, w_lo = _split_bf16(w_ref[0])
    o_ref[0] = _dot(a_hi, w_hi) + _dot(a_lo, w_hi) + _dot(a_hi, w_lo) + b_ref[0]


def _mod_call(c, w, b):
    n_l, d, n = w.shape
    bsz = c.shape[0]
    tn = 1024
    return pl.pallas_call(
        _mod_kernel,
        grid=(n_l, n // tn),
        in_specs=[pl.BlockSpec((bsz, d), lambda l, j: (0, 0)),
                  pl.BlockSpec((1, d, tn), lambda l, j: (l, 0, j)),
                  pl.BlockSpec((1, 1, tn), lambda l, j: (l, 0, j))],
        out_specs=pl.BlockSpec((1, bsz, tn), lambda l, j: (l, 0, j)),
        out_shape=jax.ShapeDtypeStruct((n_l, bsz, n), F32),
        compiler_params=_params(("arbitrary", "arbitrary"), 40),
        name="mod",
    )(c, w, b.reshape(n_l, 1, n))


def _norm_mod(x, g, shift, scale):
    ms = jnp.mean(x * x, axis=-1, keepdims=True)
    return (x * lax.rsqrt(ms + RMS_EPS) * g) * (1.0 + scale) + shift


def _ffn_kernel(*refs, mode, final):
    it = iter(refs)
    x_ref, vec_ref = next(it), next(it)
    x = x_ref[...]
    vec = vec_ref[0]
    if mode == "moba":
        attn_ref, wo_ref = next(it), next(it)
        x = x + vec[4:5] * _dot(attn_ref[...], wo_ref[...])
    elif mode == "dil":
        o_refs = [next(it) for _ in range(N_BRANCHES)]
        l_refs = [next(it) for _ in range(N_BRANCHES)]
        e_ref, wo_ref = next(it), next(it)
        lse = [r[...] for r in l_refs]
        mx = functools.reduce(jnp.maximum, lse)
        ex = [jnp.exp(l - mx) for l in lse]
        inv = 1.0 / functools.reduce(lambda a, b: a + b, ex)
        attn = None
        for e, o_ref in zip(ex, o_refs):
            w_hi, w_lo = _split_bf16(e * inv)
            w_full = _dot(w_hi, e_ref[...]) + _dot(w_lo, e_ref[...])
            term = w_full * o_ref[...].astype(F32)
            attn = term if attn is None else attn + term
        x = x + vec[4:5] * _dot(attn.astype(BF16), wo_ref[...])
    wg_ref, wu_ref, wd_ref = next(it), next(it), next(it)
    o_ref, a_ref = next(it), next(it)

    h = _norm_mod(x, vec[0:1], vec[1:2], vec[2:3]).astype(BF16)
    for c in range(D_FF // FFN_CHUNK):
        sl = slice(c * FFN_CHUNK, (c + 1) * FFN_CHUNK)
        gate = _dot(h, wg_ref[:, sl])
        up = _dot(h, wu_ref[:, sl])
        a_ref[:, sl] = (gate * jax.nn.sigmoid(gate) * up).astype(BF16)
    y = x + (0.5 * vec[3:4]) * _dot(a_ref[...], wd_ref[...])
    if final:
        ms = jnp.mean(y * y, axis=-1, keepdims=True)
        y = y * lax.rsqrt(ms + RMS_EPS) * vec[5:6]
    o_ref[...] = y


def _ffn_call(x, vec, wg, wu, wd, *, mode="none", final=False, mixer=()):
    t, d = x.shape
    bsz = vec.shape[0]
    tm = TOKEN_TILE
    tiles_per_seq = (t // bsz) // tm
    row = lambda i: (i, 0)
    in_specs = [pl.BlockSpec((tm, d), row),
                pl.BlockSpec((1, 8, d), lambda i: (i // tiles_per_seq, 0, 0))]
    if mode == "moba":
        in_specs += [pl.BlockSpec((tm, d), row), _resident((d, d))]
    elif mode == "dil":
        in_specs += [pl.BlockSpec((tm, d), row)] * N_BRANCHES
        in_specs += [pl.BlockSpec((tm, N_HEADS), row)] * N_BRANCHES
        in_specs += [_resident((N_HEADS, d)), _resident((d, d))]
    in_specs += [_resident((d, D_FF)), _resident((d, D_FF)), _resident((D_FF, d))]
    return pl.pallas_call(
        functools.partial(_ffn_kernel, mode=mode, final=final),
        grid=(t // tm,),
        in_specs=in_specs,
        out_specs=pl.BlockSpec((tm, d), row),
        out_shape=jax.ShapeDtypeStruct((t, d), F32),
        scratch_shapes=[pltpu.VMEM((tm, D_FF), BF16)],
        compiler_params=_params(("arbitrary",), 52),
        name="ffn_" + mode,
    )(x, vec, *mixer, wg, wu, wd)


def _rope(y, cos, sin, first_half):
    pieces = []
    for c in range(y.shape[1] // LANES):
        yc = y[:, c * LANES:(c + 1) * LANES]
        partner = jnp.where(first_half, pltpu.roll(yc, LANES - HEAD_DIM // 2, 1),
                            pltpu.roll(yc, HEAD_DIM // 2, 1))
        pieces.append(yc * cos + partner * sin)
    return jnp.concatenate(pieces, axis=1)


def _proj_kernel(x_ref, vec_ref, cos_ref, sin_ref, w_ref, o_ref, *, rope, scale):
    vec = vec_ref[0]
    h = _norm_mod(x_ref[...], vec[0:1], vec[1:2], vec[2:3]).astype(BF16)
    cos, sin = cos_ref[...], sin_ref[...]
    lane = lax.broadcasted_iota(jnp.int32, cos.shape, 1)
    first_half = (lane % HEAD_DIM) < HEAD_DIM // 2
    for s in range(len(rope)):
        sl = slice(s * D_MODEL, (s + 1) * D_MODEL)
        y = _dot(h, w_ref[:, sl])
        if rope[s]:
            y = _rope(y, cos, sin, first_half)
        if scale[s] != 1.0:
            y = y * scale[s]
        o_ref[:, sl] = y.astype(BF16)


def _proj_call(x, vec, cos_t, sin_t, w, *, rope, scale):
    t, d = x.shape
    n = w.shape[1]
    bsz = vec.shape[0]
    tm = TOKEN_TILE
    tiles_per_seq = (t // bsz) // tm
    row = lambda i: (i, 0)
    pos = lambda i: (i % tiles_per_seq, 0)
    return pl.pallas_call(
        functools.partial(_proj_kernel, rope=rope, scale=scale),
        grid=(t // tm,),
        in_specs=[pl.BlockSpec((tm, d), row),
                  pl.BlockSpec((1, 8, d), lambda i: (i // tiles_per_seq, 0, 0)),
                  pl.BlockSpec((tm, LANES), pos),
                  pl.BlockSpec((tm, LANES), pos),
                  _resident((d, n))],
        out_specs=pl.BlockSpec((tm, n), row),
        out_shape=jax.ShapeDtypeStruct((t, n), BF16),
        compiler_params=_params(("arbitrary",), 48),
        name="proj%d" % n,
    )(x, vec, cos_t, sin_t, w)


def _moba_kernel(q_ref, k_ref, v_ref, o_ref, vt_ref, km_ref, bias_ref):
    qb = pl.program_id(2)
    n_blocks = k_ref.shape[0]
    blk = MOBA_BLOCK

    @pl.when(qb == 0)
    def _():
        for n in range(n_blocks):
            vt_ref[n] = v_ref[n].astype(F32).T.astype(BF16)
            km_ref[pl.ds(n, 1), :] = jnp.mean(k_ref[n].astype(F32), axis=0, keepdims=True)

    km_hi, km_lo = _split_bf16(km_ref[...])
    q_t = q_ref[0].astype(F32).T
    dim = lax.broadcasted_iota(jnp.int32, q_t.shape, 0)
    cand = lax.broadcasted_iota(jnp.int32, (n_blocks, blk), 0)
    is_past = cand < qb
    key_i = lax.broadcasted_iota(jnp.int32, (blk, blk), 0)
    qry_i = lax.broadcasted_iota(jnp.int32, (blk, blk), 1)
    causal = jnp.where(key_i <= qry_i, 0.0, NEG_INF)

    q_heads = []
    for a in range(2):
        in_head = (dim >= a * HEAD_DIM) & (dim < (a + 1) * HEAD_DIM)
        q_a = jnp.where(in_head, q_t, 0.0).astype(BF16)
        q_heads.append(q_a)
        gate = _dot(km_hi, q_a) + _dot(km_lo, q_a)
        bias = jnp.zeros((n_blocks, blk), F32)
        for n in range(n_blocks):
            g_n = gate[n:n + 1, :]
            beats = (gate > g_n) | ((gate == g_n) & (cand < n))
            rank = jnp.sum(jnp.where(beats & is_past, 1.0, 0.0), axis=0, keepdims=True)
            bias = jnp.where(cand == n, jnp.where(rank < MOBA_TOPK, 0.0, NEG_INF), bias)
        bias_ref[a] = bias

    k_own = k_ref[qb]
    vt_own = vt_ref[qb]
    init = []
    for a in range(2):
        s = _dot(k_own, q_heads[a]) + causal
        m = jnp.max(s, axis=0, keepdims=True)
        p = jnp.exp(s - m)
        l = jnp.sum(p, axis=0, keepdims=True)
        acc = _dot(vt_own[a * HEAD_DIM:(a + 1) * HEAD_DIM, :], p.astype(BF16))
        init += [m, l, acc]

    def body(n, carry):
        k_n = k_ref[n]
        vt_n = vt_ref[n]
        out = []
        for a in range(2):
            m, l, acc = carry[3 * a:3 * a + 3]
            s = _dot(k_n, q_heads[a]) + bias_ref[a, pl.ds(n, 1), :]
            m_new = jnp.maximum(m, jnp.max(s, axis=0, keepdims=True))
            alpha = jnp.exp(m - m_new)
            p = jnp.exp(s - m_new)
            l = alpha * l + jnp.sum(p, axis=0, keepdims=True)
            acc = alpha * acc + _dot(vt_n[a * HEAD_DIM:(a + 1) * HEAD_DIM, :], p.astype(BF16))
            out += [m_new, l, acc]
        return tuple(out)

    res = lax.fori_loop(0, qb, body, tuple(init))
    out_t = jnp.concatenate([res[2] * (1.0 / res[1]), res[5] * (1.0 / res[4])], axis=0)
    o_ref[0] = out_t.T.astype(BF16)


def _moba_call(qkv, bsz):
    t, n = qkv.shape
    seq = t // bsz
    nb = seq // MOBA_BLOCK
    d = n // 3
    col_blocks = d // LANES
    qkv3 = qkv.reshape(bsz * nb, MOBA_BLOCK, n)
    out = pl.pallas_call(
        _moba_kernel,
        grid=(bsz, col_blocks, nb),
        in_specs=[pl.BlockSpec((1, MOBA_BLOCK, LANES), lambda b, hp, i: (b * nb + i, 0, hp)),
                  pl.BlockSpec((nb, MOBA_BLOCK, LANES), lambda b, hp, i: (b, 0, col_blocks + hp)),
                  pl.BlockSpec((nb, MOBA_BLOCK, LANES), lambda b, hp, i: (b, 0, 2 * col_blocks + hp))],
        out_specs=pl.BlockSpec((1, MOBA_BLOCK, LANES), lambda b, hp, i: (b * nb + i, 0, hp)),
        out_shape=jax.ShapeDtypeStruct((bsz * nb, MOBA_BLOCK, d), BF16),
        scratch_shapes=[pltpu.VMEM((nb, LANES, MOBA_BLOCK), BF16),
                        pltpu.VMEM((nb, LANES), F32),
                        pltpu.VMEM((2, nb, MOBA_BLOCK), F32)],
        compiler_params=_params(("arbitrary", "arbitrary", "arbitrary"), 32),
        name="moba",
    )(qkv3, qkv3, qkv3)
    return out.reshape(t, d)


def _band_kernel(q_ref, kp_ref, kc_ref, vp_ref, vc_ref, o_ref, lse_ref):
    pb = pl.program_id(2)
    w = BAND
    key_i = lax.broadcasted_iota(jnp.int32, (2 * w, 2 * w), 0)
    qry_i = lax.broadcasted_iota(jnp.int32, (2 * w, 2 * w), 1) % w
    first_key = jnp.where(pb > 0, 0, w)
    allowed = (key_i >= qry_i) & (key_i <= qry_i + w) & (key_i >= first_key)
    bias = jnp.where(allowed, 0.0, NEG_INF)
    dim = lax.broadcasted_iota(jnp.int32, (LANES, w), 0)
    for hp in range(N_HEAD_PAIRS):
        sl = slice(hp * LANES, (hp + 1) * LANES)
        q_t = q_ref[0, :, sl].astype(F32).T
        q2 = jnp.concatenate([jnp.where(dim < HEAD_DIM, q_t, 0.0),
                              jnp.where(dim >= HEAD_DIM, q_t, 0.0)], axis=1).astype(BF16)
        k_band = jnp.concatenate([kp_ref[0, :, sl], kc_ref[0, :, sl]], axis=0)
        s = _dot(k_band, q2) + bias
        m = jnp.max(s, axis=0, keepdims=True)
        p = jnp.exp(s - m)
        l = jnp.sum(p, axis=0, keepdims=True)
        v_band = jnp.concatenate([vp_ref[0, :, sl], vc_ref[0, :, sl]], axis=0)
        v_t = v_band.astype(F32).T.astype(BF16)
        o_t = _dot(v_t, p.astype(BF16)) * (1.0 / l)
        o_sel = jnp.concatenate([o_t[:HEAD_DIM, :w], o_t[HEAD_DIM:, w:]], axis=0)
        o_ref[0, :, sl] = o_sel.T.astype(BF16)
        lse = m + jnp.log(l)
        lse_ref[0, 0, pl.ds(2 * hp, 1), :] = lse[:, :w]
        lse_ref[0, 0, pl.ds(2 * hp + 1, 1), :] = lse[:, w:]


def _band_call(q, kv, g, dil, bsz):
    t = q.shape[0]
    d = D_MODEL
    seq = t // bsz
    length = seq // dil
    nb = length // BAND
    nq, nkv = q.shape[1] // d, kv.shape[1] // d
    qv = q.reshape(bsz, length, dil * q.shape[1])
    kvv = kv.reshape(bsz, length, dil * kv.shape[1])
    blk = (1, BAND, d)
    prev = lambda i: jnp.maximum(i - 1, 0)
    o, lse = pl.pallas_call(
        _band_kernel,
        grid=(bsz, dil, nb),
        in_specs=[pl.BlockSpec(blk, lambda b, r, i: (b, i, r * nq + g)),
                  pl.BlockSpec(blk, lambda b, r, i: (b, prev(i), r * nkv + 2 * g)),
                  pl.BlockSpec(blk, lambda b, r, i: (b, i, r * nkv + 2 * g)),
                  pl.BlockSpec(blk, lambda b, r, i: (b, prev(i), r * nkv + 2 * g + 1)),
                  pl.BlockSpec(blk, lambda b, r, i: (b, i, r * nkv + 2 * g + 1))],
        out_specs=[pl.BlockSpec(blk, lambda b, r, i: (b, i, r)),
                   pl.BlockSpec((1, 1, N_HEADS, BAND), lambda b, r, i: (b, r, 0, i))],
        out_shape=[jax.ShapeDtypeStruct((bsz, length, dil * d), BF16),
                   jax.ShapeDtypeStruct((bsz, dil, N_HEADS, length), F32)],
        compiler_params=_params(("arbitrary", "arbitrary", "arbitrary"), 32),
        name="band%d" % dil,
    )(qv, kvv, kvv, kvv, kvv)
    return o.reshape(t, d), lse


def _lse_token_order(lse, bsz):
    return lse.transpose(0, 3, 1, 2).reshape(-1, N_HEADS)


def _rope_tables(seq):
    inv = 1.0 / (ROPE_THETA ** (jnp.arange(0, HEAD_DIM, 2, dtype=F32) / HEAD_DIM))
    ang = jnp.arange(seq, dtype=F32)[:, None] * inv[None, :]
    cos, sin = jnp.cos(ang), jnp.sin(ang)
    reps = LANES // HEAD_DIM
    return (jnp.tile(jnp.concatenate([cos, cos], axis=1), (1, reps)),
            jnp.tile(jnp.concatenate([-sin, sin], axis=1), (1, reps)))


def _vec(bsz, *rows):
    rows = [jnp.broadcast_to(r, (bsz, D_MODEL)) for r in rows]
    rows += [jnp.zeros((bsz, D_MODEL), F32)] * (8 - len(rows))
    return jnp.stack(rows, axis=1)


def kernel(x, c, ada_w, ada_b, norm_g, ffn_w_gate, ffn_w_up, ffn_w_down, moba_w_qkv, moba_w_o,
           kv_ada_w, kv_ada_b, kv_norm_g, kv_w, dil_w_q, dil_w_o, final_g):
    bsz, seq, d = x.shape
    cos_t, sin_t = _rope_tables(seq)
    mod = _mod_call(c, ada_w, ada_b).reshape(DEPTH, bsz, 3, 3, d)
    kv_mod = _mod_call(c, kv_ada_w[None], kv_ada_b[None]).reshape(bsz, 2, d)
    head_expand = jnp.repeat(jnp.eye(N_HEADS, dtype=BF16), HEAD_DIM, axis=1)
    qk_scale = HEAD_DIM ** -0.5

    xf = x.reshape(bsz * seq, d)
    kv = None
    for layer in range(DEPTH):
        m = mod[layer]
        if layer == N_A_LAYERS:
            kv = _proj_call(xf, _vec(bsz, kv_norm_g, kv_mod[:, 0], kv_mod[:, 1]), cos_t, sin_t,
                            kv_w.astype(BF16), rope=(True, False) * N_BRANCHES,
                            scale=(1.0,) * (2 * N_BRANCHES))
        ffn_w = [w[layer, 0].astype(BF16) for w in (ffn_w_gate, ffn_w_up, ffn_w_down)]
        xf = _ffn_call(xf, _vec(bsz, norm_g[layer, 0], m[:, 0, 0], m[:, 0, 1], m[:, 0, 2]), *ffn_w)

        mix_vec = _vec(bsz, norm_g[layer, 1], m[:, 1, 0], m[:, 1, 1])
        if layer < N_A_LAYERS:
            qkv = _proj_call(xf, mix_vec, cos_t, sin_t, moba_w_qkv[layer].astype(BF16),
                             rope=(True, True, False), scale=(qk_scale, 1.0, 1.0))
            mixer = (_moba_call(qkv, bsz), moba_w_o[layer].astype(BF16))
            mode = "moba"
        else:
            lb = layer - N_A_LAYERS
            q = _proj_call(xf, mix_vec, cos_t, sin_t, dil_w_q[lb].astype(BF16),
                           rope=(True,) * N_BRANCHES, scale=(qk_scale,) * N_BRANCHES)
            outs, lses = [], []
            for g, (window, dil) in enumerate(DILATED_BRANCHES):
                assert window // dil == BAND
                o_g, lse_g = _band_call(q, kv, g, dil, bsz)
                outs.append(o_g)
                lses.append(_lse_token_order(lse_g, bsz))
            mixer = (*outs, *lses, head_expand, dil_w_o[lb].astype(BF16))
            mode = "dil"
        ffn_w = [w[layer, 1].astype(BF16) for w in (ffn_w_gate, ffn_w_up, ffn_w_down)]
        xf = _ffn_call(xf, _vec(bsz, norm_g[layer, 2], m[:, 2, 0], m[:, 2, 1], m[:, 2, 2],
                                m[:, 1, 2], final_g),
                       *ffn_w, mode=mode, final=(layer == DEPTH - 1), mixer=mixer)
    return xf.reshape(bsz, seq, d)
```

```python
import functools
import math

import jax
import jax.numpy as jnp
from jax import lax
from jax.experimental import pallas as pl
from jax.experimental.pallas import tpu as pltpu

F32 = jnp.float32
BF16 = jnp.bfloat16

D_MODEL = 1024
HEAD_DIM = 64
N_HEADS = D_MODEL // HEAD_DIM
D_FF = 2816
ROPE_THETA = 10000.0
RMS_EPS = 1e-6
DEPTH = 4
N_A_LAYERS = 2
MOBA_BLOCK = 256
MOBA_TOPK = 3
DILATED_BRANCHES = ((128, 1), (512, 4), (2048, 16))
N_BRANCHES = len(DILATED_BRANCHES)
DILATIONS = tuple(d for _, d in DILATED_BRANCHES)
BAND = 128
LANES = 128
FFN_CHUNK = 256
TOKEN_TILE = 512
BAND_HEAD_PAIRS = 4
NEG_INF = float("-inf")
LOG2E = math.log2(math.e)
LN2 = math.log(2.0)
MIB = 1024 * 1024


def _dot(a, b):
    return jnp.dot(a, b, preferred_element_type=F32)


def _split_bf16(a):
    hi = a.astype(BF16)
    lo = (a - hi.astype(F32)).astype(BF16)
    return hi, lo


def _resident(shape):
    return pl.BlockSpec(shape, lambda *_: (0,) * len(shape),
                        pipeline_mode=pl.Buffered(1))


def _fold_rows(x, op):
    while x.shape[0] > 8 and x.shape[0] % 16 == 0:
        half = x.shape[0] // 2
        x = op(x[:half], x[half:])
    return x


def _reduce_rows(pieces, op, reduce_fn):
    folded = functools.reduce(op, [_fold_rows(x, op) for x in pieces])
    return reduce_fn(folded, axis=0, keepdims=True)


def _params(semantics, vmem_mib):
    return pltpu.CompilerParams(dimension_semantics=semantics,
                                vmem_limit_bytes=vmem_mib * MIB)


def _skewed(units, stages, delays):
    results = {}
    for t in range(len(units) + delays[-1]):
        for k, (stage, delay) in enumerate(zip(stages, delays)):
            i = t - delay
            if 0 <= i < len(units):
                results[k, i] = stage(units[i], results.pop((k - 1, i), None))


def _mod_kernel(c_ref, w_ref, b_ref, o_ref):
    c = c_ref[...]
    a_hi, a_lo = _split_bf16(c * jax.nn.sigmoid(c))
    w_hi, w_lo = _split_bf16(w_ref[0])
    o_ref[0] = _dot(a_hi, w_hi) + _dot(a_lo, w_hi) + _dot(a_hi, w_lo) + b_ref[0]


def _mod_call(c, w, b):
    n_l, d, n = w.shape
    bsz = c.shape[0]
    tn = 1024
    return pl.pallas_call(
        _mod_kernel,
        grid=(n_l, n // tn),
        in_specs=[pl.BlockSpec((bsz, d), lambda l, j: (0, 0)),
                  pl.BlockSpec((1, d, tn), lambda l, j: (l, 0, j)),
                  pl.BlockSpec((1, 1, tn), lambda l, j: (l, 0, j))],
        out_specs=pl.BlockSpec((1, bsz, tn), lambda l, j: (l, 0, j)),
        out_shape=jax.ShapeDtypeStruct((n_l, bsz, n), F32),
        compiler_params=_params(("arbitrary", "arbitrary"), 40),
        name="mod",
    )(c, w, b.reshape(n_l, 1, n))


def _norm_mod(x, g, shift, scale):
    ms = jnp.mean(x * x, axis=-1, keepdims=True)
    return (x * lax.rsqrt(ms + RMS_EPS) * g) * (1.0 + scale) + shift


def _store_lane_chunks(ref3, x):
    for c in range(ref3.shape[0]):
        ref3[c] = x[:, c * LANES:(c + 1) * LANES]


def _load_lane_chunks(ref3):
    return jnp.concatenate([ref3[c] for c in range(ref3.shape[0])], axis=1)


def _load_strided_order(ref3, dil):
    rows = ref3.shape[1] // dil
    cols = [jnp.concatenate([ref3[c, pl.ds(r, rows, stride=dil), :] for r in range(dil)], axis=0)
            for c in range(ref3.shape[0])]
    return jnp.concatenate(cols, axis=1)


def _store_token_order(ref3, block_ref, dil):
    rows = block_ref.shape[2]
    for r in range(dil):
        piece = block_ref[0, r].astype(F32)
        for c in range(ref3.shape[0]):
            ref3[c, pl.ds(r, rows, stride=dil), :] = piece[:, c * LANES:(c + 1) * LANES]


def _ffn_kernel(*refs, mode, final):
    it = iter(refs)
    x_ref, vec_ref = next(it), next(it)
    x = x_ref[...]
    vec = vec_ref[0]
    if mode == "moba":
        attn_ref, wo_ref = next(it), next(it)
        x = x + vec[4:5] * _dot(attn_ref[...], wo_ref[...])
    elif mode == "dil":
        o_refs = [next(it) for _ in range(N_BRANCHES)]
        l_refs = [next(it) for _ in range(N_BRANCHES)]
        e_ref, wo_ref = next(it), next(it)
    wg_ref, wu_ref, wd_ref = next(it), next(it), next(it)
    o_ref, a_ref = next(it), next(it)
    if mode == "dil":
        u_ref = next(it)
        lse = [r[...] for r in l_refs]
        mx = functools.reduce(jnp.maximum, lse)
        ex = [jnp.exp(l - mx) for l in lse]
        inv = 1.0 / functools.reduce(lambda a, b: a + b, ex)
        attn = None
        for e, br_ref, dil in zip(ex, o_refs, DILATIONS):
            if dil == 1:
                o_g = br_ref[...].astype(F32)
            else:
                _store_token_order(u_ref, br_ref, dil)
                o_g = _load_lane_chunks(u_ref)
            w_hi, w_lo = _split_bf16(e * inv)
            w_full = _dot(w_hi, e_ref[...]) + _dot(w_lo, e_ref[...])
            term = w_full * o_g
            attn = term if attn is None else attn + term
        x = x + vec[4:5] * _dot(attn.astype(BF16), wo_ref[...])

    h = _norm_mod(x, vec[0:1], vec[1:2], vec[2:3]).astype(BF16)
    for c in range(D_FF // FFN_CHUNK):
        sl = slice(c * FFN_CHUNK, (c + 1) * FFN_CHUNK)
        gate = _dot(h, wg_ref[:, sl])
        up = _dot(h, wu_ref[:, sl])
        a_ref[:, sl] = (gate * jax.nn.sigmoid(gate) * up).astype(BF16)
    y = x + (0.5 * vec[3:4]) * _dot(a_ref[...], wd_ref[...])
    if final:
        ms = jnp.mean(y * y, axis=-1, keepdims=True)
        y = y * lax.rsqrt(ms + RMS_EPS) * vec[5:6]
    o_ref[...] = y


def _strided_block_spec(dil, tm, tiles_per_seq):
    return pl.BlockSpec((1, dil, tm // dil, D_MODEL),
                        lambda i: (i // tiles_per_seq, 0, i % tiles_per_seq, 0))


def _ffn_call(x, vec, wg, wu, wd, *, mode="none", final=False, mixer=()):
    t, d = x.shape
    bsz = vec.shape[0]
    tm = TOKEN_TILE
    tiles_per_seq = (t // bsz) // tm
    row = lambda i: (i, 0)
    in_specs = [pl.BlockSpec((tm, d), row),
                pl.BlockSpec((1, 8, d), lambda i: (i // tiles_per_seq, 0, 0))]
    scratch = [pltpu.VMEM((tm, D_FF), BF16)]
    if mode == "moba":
        in_specs += [pl.BlockSpec((tm, d), row), _resident((d, d))]
    elif mode == "dil":
        for dil in DILATIONS:
            in_specs.append(pl.BlockSpec((tm, d), row) if dil == 1
                            else _strided_block_spec(dil, tm, tiles_per_seq))
        in_specs += [pl.BlockSpec((tm, N_HEADS), row)] * N_BRANCHES
        in_specs += [_resident((N_HEADS, d)), _resident((d, d))]
        scratch.append(pltpu.VMEM((d // LANES, tm, LANES), F32))
    in_specs += [_resident((d, D_FF)), _resident((d, D_FF)), _resident((D_FF, d))]
    return pl.pallas_call(
        functools.partial(_ffn_kernel, mode=mode, final=final),
        grid=(t // tm,),
        in_specs=in_specs,
        out_specs=pl.BlockSpec((tm, d), row),
        out_shape=jax.ShapeDtypeStruct((t, d), F32),
        scratch_shapes=scratch,
        compiler_params=_params(("arbitrary",), 52),
        name="ffn_" + mode,
    )(x, vec, *mixer, wg, wu, wd)


def _rope(y, cos, sin, first_half):
    pieces = []
    for c in range(y.shape[1] // LANES):
        yc = y[:, c * LANES:(c + 1) * LANES]
        partner = jnp.where(first_half, pltpu.roll(yc, LANES - HEAD_DIM // 2, 1),
                            pltpu.roll(yc, HEAD_DIM // 2, 1))
        pieces.append(yc * cos + partner * sin)
    return jnp.concatenate(pieces, axis=1)


def _proj_kernel(x_ref, vec_ref, cos_ref, sin_ref, w_ref, *rest, sections, fused_out):
    o_refs, h_ref = rest[:-1], rest[-1]
    tm = x_ref.shape[0]
    vec = vec_ref[0]
    h = _norm_mod(x_ref[...], vec[0:1], vec[1:2], vec[2:3])
    lane = lax.broadcasted_iota(jnp.int32, (tm, LANES), 1)
    first_half = (lane % HEAD_DIM) < HEAD_DIM // 2
    dils = sorted({dil for _, _, dil in sections})
    if dils != [1]:
        _store_lane_chunks(h_ref, h)
    h_by_dil = {dil: (h if dil == 1 else _load_strided_order(h_ref, dil)).astype(BF16) for dil in dils}
    for s, (rope, scale, dil) in enumerate(sections):
        y = _dot(h_by_dil[dil], w_ref[:, s * D_MODEL:(s + 1) * D_MODEL])
        if rope:
            t_i = dils.index(dil)
            y = _rope(y, cos_ref[t_i], sin_ref[t_i], first_half)
        if scale != 1.0:
            y = y * scale
        y = y.astype(BF16)
        if fused_out:
            o_refs[0][:, s * D_MODEL:(s + 1) * D_MODEL] = y
        elif dil == 1:
            o_refs[s][...] = y
        else:
            rows = tm // dil
            for r in range(dil):
                o_refs[s][0, r] = y[r * rows:(r + 1) * rows, :]


def _proj_call(x, vec, tables, w, *, sections, fused_out, bsz):
    t, d = x.shape
    n = w.shape[1]
    tm = TOKEN_TILE
    seq = t // bsz
    tiles_per_seq = seq // tm
    row = lambda i: (i, 0)
    n_tab = tables[0].shape[0]
    tab_spec = pl.BlockSpec((n_tab, tm, LANES), lambda i: (0, i % tiles_per_seq, 0))
    if fused_out:
        out_specs = [pl.BlockSpec((tm, n), row)]
        out_shape = [jax.ShapeDtypeStruct((t, n), BF16)]
    else:
        out_specs, out_shape = [], []
        for _, _, dil in sections:
            if dil == 1:
                out_specs.append(pl.BlockSpec((tm, d), row))
                out_shape.append(jax.ShapeDtypeStruct((t, d), BF16))
            else:
                out_specs.append(_strided_block_spec(dil, tm, tiles_per_seq))
                out_shape.append(jax.ShapeDtypeStruct((bsz, dil, seq // dil, d), BF16))
    return pl.pallas_call(
        functools.partial(_proj_kernel, sections=sections, fused_out=fused_out),
        grid=(t // tm,),
        in_specs=[pl.BlockSpec((tm, d), row),
                  pl.BlockSpec((1, 8, d), lambda i: (i // tiles_per_seq, 0, 0)),
                  tab_spec, tab_spec,
                  _resident((d, n))],
        out_specs=out_specs,
        out_shape=out_shape,
        scratch_shapes=[pltpu.VMEM((d // LANES, tm, LANES), F32)],
        compiler_params=_params(("arbitrary",), 48),
        name="proj%d" % n,
    )(x, vec, *tables, w)


def _moba_kernel(q_ref, k_ref, v_ref, o_ref, vt_ref):
    blk = MOBA_BLOCK
    n_blocks = q_ref.shape[0] // blk
    rows = lambda n: slice(n * blk, (n + 1) * blk)

    k_means = []
    for n in range(n_blocks):
        vt_ref[:, rows(n)] = v_ref[rows(n), :].astype(F32).T.astype(BF16)
        k_means.append(jnp.mean(k_ref[rows(n), :].astype(F32), axis=0, keepdims=True))
    km_hi, km_lo = _split_bf16(jnp.concatenate(k_means, axis=0))

    dim = lax.broadcasted_iota(jnp.int32, (LANES, blk), 0)
    in_head = [dim < HEAD_DIM, dim >= HEAD_DIM]
    cand = lax.broadcasted_iota(jnp.int32, (n_blocks, blk), 0)
    piece = blk // 2
    key_i = lax.broadcasted_iota(jnp.int32, (piece, blk), 0)
    qry_i = lax.broadcasted_iota(jnp.int32, (piece, blk), 1)
    causal = [jnp.where(key_i + h * piece <= qry_i, 0.0, NEG_INF) for h in range(2)]

    def load_q(qb, _):
        q_t = q_ref[rows(qb), :].astype(F32).T
        return [jnp.where(in_head[a], q_t, 0.0).astype(BF16) for a in range(2)]

    def scores(qb, q_heads):
        out = []
        for q_a in q_heads:
            s_all = _dot(k_ref[0:(qb + 1) * blk, :], q_a)
            s = [s_all[i * piece:(i + 1) * piece, :] for i in range(2 * (qb + 1))]
            gate = _dot(km_hi, q_a) + _dot(km_lo, q_a) if qb > MOBA_TOPK else None
            out.append((s, gate))
        return out

    def softmax_pv(qb, stage):
        return [head_softmax_pv(qb, a, *stage[a]) for a in range(2)]

    def store(qb, heads):
        o_ref[rows(qb), :] = jnp.concatenate(heads, axis=0).T.astype(BF16)

    def head_softmax_pv(qb, a, s, gate):
        s = list(s)
        if gate is not None:
            for n in range(qb):
                g_n = gate[n:n + 1, :]
                beats = ((gate > g_n) | ((gate == g_n) & (cand < n))) & (cand < qb)
                rank = jnp.sum(jnp.where(beats, 1.0, 0.0), axis=0, keepdims=True)
                unselected = jnp.where(rank < MOBA_TOPK, 0.0, NEG_INF)
                for h in range(2):
                    s[2 * n + h] = s[2 * n + h] + unselected
        for h in range(2):
            s[2 * qb + h] = s[2 * qb + h] + causal[h]
        m = _reduce_rows(s, jnp.maximum, jnp.max)
        p_pieces = [jnp.exp2(x - m) for x in s]
        l = _reduce_rows(p_pieces, jnp.add, jnp.sum)
        p = jnp.concatenate([x.astype(BF16) for x in p_pieces], axis=0)
        acc = _dot(vt_ref[a * HEAD_DIM:(a + 1) * HEAD_DIM, 0:(qb + 1) * blk], p)
        return acc * (1.0 / l)

    _skewed(list(range(n_blocks)), (load_q, scores, softmax_pv, store), (0, 1, 2, 3))


def _moba_call(qkv, bsz):
    t, n = qkv.shape
    seq = t // bsz
    d = n // 3
    col_blocks = d // LANES
    return pl.pallas_call(
        _moba_kernel,
        grid=(bsz, col_blocks),
        in_specs=[pl.BlockSpec((seq, LANES), lambda b, hp: (b, hp)),
                  pl.BlockSpec((seq, LANES), lambda b, hp: (b, col_blocks + hp)),
                  pl.BlockSpec((seq, LANES), lambda b, hp: (b, 2 * col_blocks + hp))],
        out_specs=pl.BlockSpec((seq, LANES), lambda b, hp: (b, hp)),
        out_shape=jax.ShapeDtypeStruct((t, d), BF16),
        scratch_shapes=[pltpu.VMEM((LANES, seq), BF16)],
        compiler_params=_params(("arbitrary", "arbitrary"), 40),
        name="moba",
    )(qkv, qkv, qkv)


def _band_kernel(q_ref, k_ref, v_ref, o_ref, lse_ref, *, blocks_per_seq):
    w = BAND
    n_blocks = q_ref.shape[0] // w
    n_pairs = q_ref.shape[1] // LANES
    key_i = lax.broadcasted_iota(jnp.int32, (w, 2 * w), 0)
    qry_i = lax.broadcasted_iota(jnp.int32, (w, 2 * w), 1) % w
    bias_of = {0: jnp.where(key_i <= qry_i, 0.0, NEG_INF), 1: jnp.where(key_i >= qry_i, 0.0, NEG_INF)}
    dim = lax.broadcasted_iota(jnp.int32, (LANES, w), 0)
    rows = lambda j: slice(j * w, (j + 1) * w)
    cols = lambda hp: slice(hp * LANES, (hp + 1) * LANES)

    def back_blocks(j):
        return (0,) if j % blocks_per_seq == 0 else (0, 1)

    v_t = {}

    def transposes(unit, _):
        j, hp = unit
        v_t[unit] = v_ref[rows(j), cols(hp)].astype(F32).T.astype(BF16)
        q_t = q_ref[rows(j), cols(hp)].astype(F32).T
        return jnp.concatenate([jnp.where(dim < HEAD_DIM, q_t, 0.0),
                                jnp.where(dim >= HEAD_DIM, q_t, 0.0)], axis=1).astype(BF16)

    def scores(unit, q2):
        j, hp = unit
        return [_dot(k_ref[rows(j - back), cols(hp)], q2) for back in back_blocks(j)]

    def softmax_pv(unit, s_blocks):
        j, hp = unit
        s_blocks = [s + bias_of[back] for s, back in zip(s_blocks, back_blocks(j))]
        m = _reduce_rows(s_blocks, jnp.maximum, jnp.max)
        p_blocks = [jnp.exp2(s - m) for s in s_blocks]
        l = _reduce_rows(p_blocks, jnp.add, jnp.sum)
        o_t = functools.reduce(jnp.add, [_dot(v_t[j - back, hp], p.astype(BF16))
                                         for p, back in zip(p_blocks, back_blocks(j))]) * (1.0 / l)
        if j % blocks_per_seq != 0:
            del v_t[j - 1, hp]
        if (j + 1) % blocks_per_seq == 0:
            del v_t[unit]
        lse = (m + jnp.log2(l)) * LN2
        lse_ref[0, 2 * hp:2 * hp + 1, rows(j)] = lse[:, :w]
        lse_ref[0, 2 * hp + 1:2 * hp + 2, rows(j)] = lse[:, w:]
        return jnp.concatenate([o_t[:HEAD_DIM, :w], o_t[HEAD_DIM:, w:]], axis=0)

    def store(unit, o_sel):
        j, hp = unit
        o_ref[rows(j), cols(hp)] = o_sel.T.astype(BF16)

    units = [(j, hp) for j in range(n_blocks) for hp in range(n_pairs)]
    _skewed(units, (transposes, scores, softmax_pv, store), (0, 3, 6, 9))


def _band_call(q, k, v, dil, bsz):
    t, d = q.shape
    seq = t // bsz
    cw = BAND_HEAD_PAIRS * LANES
    blocks_per_seq = (seq // dil) // BAND
    blk = pl.BlockSpec((seq, cw), lambda b, c: (b, c))
    return pl.pallas_call(
        functools.partial(_band_kernel, blocks_per_seq=blocks_per_seq),
        grid=(bsz, d // cw),
        in_specs=[blk, blk, blk],
        out_specs=[blk, pl.BlockSpec((1, 2 * BAND_HEAD_PAIRS, seq), lambda b, c: (b, c, 0))],
        out_shape=[jax.ShapeDtypeStruct((t, d), BF16),
                   jax.ShapeDtypeStruct((bsz, N_HEADS, seq), F32)],
        compiler_params=_params(("arbitrary", "arbitrary"), 40),
        name="band%d" % dil,
    )(q, k, v)


def _lse_token_order(lse, dil):
    bsz, n_h, seq = lse.shape
    return lse.reshape(bsz, n_h, dil, seq // dil).transpose(0, 3, 2, 1).reshape(bsz * seq, n_h)


def _rope_tables(seq):
    inv = 1.0 / (ROPE_THETA ** (jnp.arange(0, HEAD_DIM, 2, dtype=F32) / HEAD_DIM))
    ang = jnp.arange(seq, dtype=F32)[:, None] * inv[None, :]
    cos, sin = jnp.cos(ang), jnp.sin(ang)
    reps = LANES // HEAD_DIM
    tabs = (jnp.tile(jnp.concatenate([cos, cos], axis=1), (1, reps)),
            jnp.tile(jnp.concatenate([-sin, sin], axis=1), (1, reps)))

    def strided(tab, dil):
        tiles = seq // TOKEN_TILE
        return tab.reshape(tiles, TOKEN_TILE // dil, dil, LANES).transpose(0, 2, 1, 3).reshape(seq, LANES)

    return tuple(jnp.stack([strided(tab, dil) for dil in DILATIONS]) for tab in tabs)


def _vec(bsz, *rows):
    rows = [jnp.broadcast_to(r, (bsz, D_MODEL)) for r in rows]
    rows += [jnp.zeros((bsz, D_MODEL), F32)] * (8 - len(rows))
    return jnp.stack(rows, axis=1)


def kernel(x, c, ada_w, ada_b, norm_g, ffn_w_gate, ffn_w_up, ffn_w_down, moba_w_qkv, moba_w_o,
           kv_ada_w, kv_ada_b, kv_norm_g, kv_w, dil_w_q, dil_w_o, final_g):
    bsz, seq, d = x.shape
    t = bsz * seq
    tables = _rope_tables(seq)
    natural_tables = tuple(tab[:1] for tab in tables)
    mod = _mod_call(c, ada_w, ada_b).reshape(DEPTH, bsz, 3, 3, d)
    kv_mod = _mod_call(c, kv_ada_w[None], kv_ada_b[None]).reshape(bsz, 2, d)
    head_expand = jnp.repeat(jnp.eye(N_HEADS, dtype=BF16), HEAD_DIM, axis=1)
    q_scale = HEAD_DIM ** -0.5 * LOG2E

    xf = x.reshape(t, d)
    kvs = None
    for layer in range(DEPTH):
        m = mod[layer]
        if layer == N_A_LAYERS:
            kv_sections = tuple((is_k, 1.0, dil) for dil in DILATIONS for is_k in (True, False))
            kvs = _proj_call(xf, _vec(bsz, kv_norm_g, kv_mod[:, 0], kv_mod[:, 1]), tables,
                             kv_w.astype(BF16), sections=kv_sections, fused_out=False, bsz=bsz)
        ffn_w = [w[layer, 0].astype(BF16) for w in (ffn_w_gate, ffn_w_up, ffn_w_down)]
        xf = _ffn_call(xf, _vec(bsz, norm_g[layer, 0], m[:, 0, 0], m[:, 0, 1], m[:, 0, 2]), *ffn_w)

        mix_vec = _vec(bsz, norm_g[layer, 1], m[:, 1, 0], m[:, 1, 1])
        if layer < N_A_LAYERS:
            qkv, = _proj_call(xf, mix_vec, natural_tables, moba_w_qkv[layer].astype(BF16),
                              sections=((True, q_scale, 1), (True, 1.0, 1), (False, 1.0, 1)),
                              fused_out=True, bsz=bsz)
            mixer = (_moba_call(qkv, bsz), moba_w_o[layer].astype(BF16))
            mode = "moba"
        else:
            lb = layer - N_A_LAYERS
            qs = _proj_call(xf, mix_vec, tables, dil_w_q[lb].astype(BF16),
                            sections=tuple((True, q_scale, dil) for dil in DILATIONS),
                            fused_out=False, bsz=bsz)
            outs, lses = [], []
            for g, (window, dil) in enumerate(DILATED_BRANCHES):
                assert window // dil == BAND
                o_g, lse_g = _band_call(qs[g].reshape(t, d), kvs[2 * g].reshape(t, d),
                                        kvs[2 * g + 1].reshape(t, d), dil, bsz)
                outs.append(o_g if dil == 1 else o_g.reshape(bsz, dil, seq // dil, d))
                lses.append(_lse_token_order(lse_g, dil))
            mixer = (*outs, *lses, head_expand, dil_w_o[lb].astype(BF16))
            mode = "dil"
        ffn_w = [w[layer, 1].astype(BF16) for w in (ffn_w_gate, ffn_w_up, ffn_w_down)]
        xf = _ffn_call(xf, _vec(bsz, norm_g[layer, 2], m[:, 2, 0], m[:, 2, 1], m[:, 2, 2],
                                m[:, 1, 2], final_g),
                       *ffn_w, mode=mode, final=(layer == DEPTH - 1), mixer=mixer)
    return xf.reshape(bsz, seq, d)
```

```python
import functools
import math

import jax
import jax.numpy as jnp
from jax import lax
from jax.experimental import pallas as pl
from jax.experimental.pallas import tpu as pltpu

F32 = jnp.float32
BF16 = jnp.bfloat16

D_MODEL = 1024
HEAD_DIM = 64
N_HEADS = D_MODEL // HEAD_DIM
D_FF = 2816
ROPE_THETA = 10000.0
RMS_EPS = 1e-6
DEPTH = 4
N_A_LAYERS = 2
MOBA_BLOCK = 256
MOBA_TOPK = 3
MOBA_PIECE = 128
ONES_ROWS = 16
MOBA_HEAD_PAIRS = 2
DILATED_BRANCHES = ((128, 1), (512, 4), (2048, 16))
N_BRANCHES = len(DILATED_BRANCHES)
DILATIONS = tuple(d for _, d in DILATED_BRANCHES)
BAND = 128
LANES = 128
FFN_CHUNK = 256
TOKEN_TILE = 512
BAND_HEAD_PAIRS = 4
NEG_INF = float("-inf")
LOG2E = math.log2(math.e)
LN2 = math.log(2.0)
MIB = 1024 * 1024


def _dot(a, b):
    return jnp.dot(a, b, preferred_element_type=F32)


def _split_bf16(a):
    hi = a.astype(BF16)
    lo = (a - hi.astype(F32)).astype(BF16)
    return hi, lo


def _resident(shape):
    return pl.BlockSpec(shape, lambda *_: (0,) * len(shape),
                        pipeline_mode=pl.Buffered(1))


def _fold_rows(x, op):
    while x.shape[0] > 8 and x.shape[0] % 16 == 0:
        half = x.shape[0] // 2
        x = op(x[:half], x[half:])
    return x


def _reduce_rows(pieces, op, reduce_fn):
    folded = functools.reduce(op, [_fold_rows(x, op) for x in pieces])
    return reduce_fn(folded, axis=0, keepdims=True)


def _params(semantics, vmem_mib):
    return pltpu.CompilerParams(dimension_semantics=semantics,
                                vmem_limit_bytes=vmem_mib * MIB)


def _skewed(units, stages, delays):
    results = {}
    for t in range(len(units) + delays[-1]):
        for k, (stage, delay) in enumerate(zip(stages, delays)):
            i = t - delay
            if 0 <= i < len(units):
                results[k, i] = stage(units[i], results.pop((k - 1, i), None))


def _mod_kernel(c_ref, w_ref, b_ref, o_ref):
    c = c_ref[...]
    a_hi, a_lo = _split_bf16(c * jax.nn.sigmoid(c))
    w_hi, w_lo = _split_bf16(w_ref[0])
    o_ref[0] = _dot(a_hi, w_hi) + _dot(a_lo, w_hi) + _dot(a_hi, w_lo) + b_ref[0]


def _mod_call(c, w, b):
    n_l, d, n = w.shape
    bsz = c.shape[0]
    tn = 1024
    return pl.pallas_call(
        _mod_kernel,
        grid=(n_l, n // tn),
        in_specs=[pl.BlockSpec((bsz, d), lambda l, j: (0, 0)),
                  pl.BlockSpec((1, d, tn), lambda l, j: (l, 0, j)),
                  pl.BlockSpec((1, 1, tn), lambda l, j: (l, 0, j))],
        out_specs=pl.BlockSpec((1, bsz, tn), lambda l, j: (l, 0, j)),
        out_shape=jax.ShapeDtypeStruct((n_l, bsz, n), F32),
        compiler_params=_params(("arbitrary", "arbitrary"), 40),
        name="mod",
    )(c, w, b.reshape(n_l, 1, n))


def _norm_mod(x, g, shift, scale):
    ms = jnp.mean(x * x, axis=-1, keepdims=True)
    return (x * lax.rsqrt(ms + RMS_EPS) * g) * (1.0 + scale) + shift


def _store_lane_chunks(ref3, x):
    for c in range(ref3.shape[0]):
        ref3[c] = x[:, c * LANES:(c + 1) * LANES]


def _load_lane_chunks(ref3):
    return jnp.concatenate([ref3[c] for c in range(ref3.shape[0])], axis=1)


def _load_strided_order(ref3, dil):
    rows = ref3.shape[1] // dil
    cols = [jnp.concatenate([ref3[c, pl.ds(r, rows, stride=dil), :] for r in range(dil)], axis=0)
            for c in range(ref3.shape[0])]
    return jnp.concatenate(cols, axis=1)


def _store_token_order(ref3, block_ref, dil):
    rows = block_ref.shape[2]
    for r in range(dil):
        piece = block_ref[0, r].astype(F32)
        for c in range(ref3.shape[0]):
            ref3[c, pl.ds(r, rows, stride=dil), :] = piece[:, c * LANES:(c + 1) * LANES]


def _ffn_kernel(*refs, mode, final):
    it = iter(refs)
    x_ref, vec_ref = next(it), next(it)
    x = x_ref[...]
    vec = vec_ref[0]
    if mode == "moba":
        attn_ref, wo_ref = next(it), next(it)
        x = x + vec[4:5] * _dot(attn_ref[...], wo_ref[...])
    elif mode == "dil":
        o_refs = [next(it) for _ in range(N_BRANCHES)]
        l_refs = [next(it) for _ in range(N_BRANCHES)]
        e_ref, wo_ref = next(it), next(it)
    wg_ref, wu_ref, wd_ref = next(it), next(it), next(it)
    o_ref, a_ref = next(it), next(it)
    if mode == "dil":
        u_ref = next(it)
        lse = [r[...] for r in l_refs]
        mx = functools.reduce(jnp.maximum, lse)
        ex = [jnp.exp(l - mx) for l in lse]
        inv = 1.0 / functools.reduce(lambda a, b: a + b, ex)
        attn = None
        for e, br_ref, dil in zip(ex, o_refs, DILATIONS):
            if dil == 1:
                o_g = br_ref[...].astype(F32)
            else:
                _store_token_order(u_ref, br_ref, dil)
                o_g = _load_lane_chunks(u_ref)
            w_full = _dot(jnp.concatenate(_split_bf16(e * inv), axis=1), e_ref[...])
            term = w_full * o_g
            attn = term if attn is None else attn + term
        x = x + vec[4:5] * _dot(attn.astype(BF16), wo_ref[...])

    h = _norm_mod(x, vec[0:1], vec[1:2], vec[2:3]).astype(BF16)
    for c in range(D_FF // FFN_CHUNK):
        sl = slice(c * FFN_CHUNK, (c + 1) * FFN_CHUNK)
        gate = _dot(h, wg_ref[:, sl])
        up = _dot(h, wu_ref[:, sl])
        a_ref[:, sl] = (gate * jax.nn.sigmoid(gate) * up).astype(BF16)
    y = x + (0.5 * vec[3:4]) * _dot(a_ref[...], wd_ref[...])
    if final:
        ms = jnp.mean(y * y, axis=-1, keepdims=True)
        y = y * lax.rsqrt(ms + RMS_EPS) * vec[5:6]
    o_ref[...] = y


def _strided_block_spec(dil, tm, tiles_per_seq):
    return pl.BlockSpec((1, dil, tm // dil, D_MODEL),
                        lambda i: (i // tiles_per_seq, 0, i % tiles_per_seq, 0))


def _ffn_call(x, vec, wg, wu, wd, *, mode="none", final=False, mixer=()):
    t, d = x.shape
    bsz = vec.shape[0]
    tm = TOKEN_TILE
    tiles_per_seq = (t // bsz) // tm
    row = lambda i: (i, 0)
    in_specs = [pl.BlockSpec((tm, d), row),
                pl.BlockSpec((1, 8, d), lambda i: (i // tiles_per_seq, 0, 0))]
    scratch = [pltpu.VMEM((tm, D_FF), BF16)]
    if mode == "moba":
        in_specs += [pl.BlockSpec((tm, d), row), _resident((d, d))]
    elif mode == "dil":
        for dil in DILATIONS:
            in_specs.append(pl.BlockSpec((tm, d), row) if dil == 1
                            else _strided_block_spec(dil, tm, tiles_per_seq))
        in_specs += [pl.BlockSpec((tm, N_HEADS), row)] * N_BRANCHES
        in_specs += [_resident((2 * N_HEADS, d)), _resident((d, d))]
        scratch.append(pltpu.VMEM((d // LANES, tm, LANES), F32))
    in_specs += [_resident((d, D_FF)), _resident((d, D_FF)), _resident((D_FF, d))]
    return pl.pallas_call(
        functools.partial(_ffn_kernel, mode=mode, final=final),
        grid=(t // tm,),
        in_specs=in_specs,
        out_specs=pl.BlockSpec((tm, d), row),
        out_shape=jax.ShapeDtypeStruct((t, d), F32),
        scratch_shapes=scratch,
        compiler_params=_params(("arbitrary",), 52),
        name="ffn_" + mode,
    )(x, vec, *mixer, wg, wu, wd)


def _in_head(dim, a):
    return (dim // (HEAD_DIM // 2)) % 2 == a


def _rope(y, cos, sin):
    pieces = []
    for c in range(y.shape[1] // LANES):
        yc = y[:, c * LANES:(c + 1) * LANES]
        pieces.append(yc * cos + pltpu.roll(yc, LANES // 2, 1) * sin)
    return jnp.concatenate(pieces, axis=1)


def _proj_kernel(x_ref, vec_ref, cos_ref, sin_ref, w_ref, *rest, sections, fused_out):
    o_refs, h_ref = rest[:-1], rest[-1]
    tm = x_ref.shape[0]
    vec = vec_ref[0]
    h = _norm_mod(x_ref[...], vec[0:1], vec[1:2], vec[2:3])
    dils = sorted({dil for _, _, dil in sections})
    if dils != [1]:
        _store_lane_chunks(h_ref, h)
    h_by_dil = {dil: (h if dil == 1 else _load_strided_order(h_ref, dil)).astype(BF16) for dil in dils}
    for s, (rope, scale, dil) in enumerate(sections):
        y = _dot(h_by_dil[dil], w_ref[:, s * D_MODEL:(s + 1) * D_MODEL])
        if rope:
            t_i = dils.index(dil)
            y = _rope(y, cos_ref[t_i], sin_ref[t_i])
        if scale != 1.0:
            y = y * scale
        y = y.astype(BF16)
        if fused_out:
            o_refs[0][:, s * D_MODEL:(s + 1) * D_MODEL] = y
        elif dil == 1:
            o_refs[s][...] = y
        else:
            rows = tm // dil
            for r in range(dil):
                o_refs[s][0, r] = y[r * rows:(r + 1) * rows, :]


def _proj_call(x, vec, tables, w, *, sections, fused_out, bsz):
    t, d = x.shape
    n = w.shape[1]
    tm = TOKEN_TILE
    seq = t // bsz
    tiles_per_seq = seq // tm
    row = lambda i: (i, 0)
    n_tab = tables[0].shape[0]
    tab_spec = pl.BlockSpec((n_tab, tm, LANES), lambda i: (0, i % tiles_per_seq, 0))
    if fused_out:
        out_specs = [pl.BlockSpec((tm, n), row)]
        out_shape = [jax.ShapeDtypeStruct((t, n), BF16)]
    else:
        out_specs, out_shape = [], []
        for _, _, dil in sections:
            if dil == 1:
                out_specs.append(pl.BlockSpec((tm, d), row))
                out_shape.append(jax.ShapeDtypeStruct((t, d), BF16))
            else:
                out_specs.append(_strided_block_spec(dil, tm, tiles_per_seq))
                out_shape.append(jax.ShapeDtypeStruct((bsz, dil, seq // dil, d), BF16))
    return pl.pallas_call(
        functools.partial(_proj_kernel, sections=sections, fused_out=fused_out),
        grid=(t // tm,),
        in_specs=[pl.BlockSpec((tm, d), row),
                  pl.BlockSpec((1, 8, d), lambda i: (i // tiles_per_seq, 0, 0)),
                  tab_spec, tab_spec,
                  _resident((d, n))],
        out_specs=out_specs,
        out_shape=out_shape,
        scratch_shapes=[pltpu.VMEM((d // LANES, tm, LANES), F32)],
        compiler_params=_params(("arbitrary",), 48),
        name="proj%d" % n,
    )(x, vec, *tables, w)


def _moba_kernel(q_ref, k_ref, v_ref, o_ref, vt_ref):
    blk = MOBA_BLOCK
    n_blocks = q_ref.shape[0] // blk
    n_pairs = q_ref.shape[1] // LANES
    rows = lambda n: slice(n * blk, (n + 1) * blk)
    cols = lambda hp: slice(hp * LANES, (hp + 1) * LANES)

    k_mean = []
    for hp in range(n_pairs):
        block_means = []
        for a in range(2):
            vt_ref[2 * hp + a, HEAD_DIM:, :] = jnp.ones((ONES_ROWS, vt_ref.shape[2]), BF16)
        for n in range(n_blocks):
            v_t = v_ref[rows(n), cols(hp)].astype(F32).T.astype(BF16)
            for a in range(2):
                vt_ref[2 * hp + a, :HEAD_DIM, rows(n)] = v_t[a * HEAD_DIM:(a + 1) * HEAD_DIM, :]
            block_means.append(jnp.mean(k_ref[rows(n), cols(hp)].astype(F32), axis=0, keepdims=True))
        k_mean.append(_split_bf16(jnp.concatenate(block_means, axis=0)))

    dim = lax.broadcasted_iota(jnp.int32, (LANES, blk), 0)
    in_head = [_in_head(dim, a) for a in range(2)]
    cand = lax.broadcasted_iota(jnp.int32, (n_blocks, blk), 0)
    piece = MOBA_PIECE
    per_block = blk // piece
    key_i = lax.broadcasted_iota(jnp.int32, (piece, blk), 0)
    qry_i = lax.broadcasted_iota(jnp.int32, (piece, blk), 1)
    causal = [jnp.where(key_i + h * piece <= qry_i, 0.0, NEG_INF) for h in range(per_block)]

    def load_q(unit, _):
        qb, hp = unit
        q_t = q_ref[rows(qb), cols(hp)].astype(F32).T
        return [jnp.where(in_head[a], q_t, 0.0).astype(BF16) for a in range(2)]

    def scores(unit, q_heads):
        qb, hp = unit
        km_hi, km_lo = k_mean[hp]
        out = []
        for q_a in q_heads:
            s_all = _dot(k_ref[0:(qb + 1) * blk, cols(hp)], q_a)
            s = [s_all[i * piece:(i + 1) * piece, :] for i in range(per_block * (qb + 1))]
            gate = _dot(km_hi, q_a) + _dot(km_lo, q_a) if qb > MOBA_TOPK else None
            out.append((s, gate))
        return out

    def softmax_pv(unit, stage):
        qb, hp = unit
        return [head_softmax_pv(qb, 2 * hp + a, *stage[a]) for a in range(2)]

    def store(unit, heads):
        qb, hp = unit
        o_ref[rows(qb), cols(hp)] = jnp.concatenate(heads, axis=0).T.astype(BF16)

    def head_softmax_pv(qb, a, s, gate):
        s = list(s)
        if gate is not None:
            for n in range(qb):
                g_n = gate[n:n + 1, :]
                beats = ((gate > g_n) | ((gate == g_n) & (cand < n))) & (cand < qb)
                rank = jnp.sum(jnp.where(beats, 1.0, 0.0), axis=0, keepdims=True)
                unselected = jnp.where(rank < MOBA_TOPK, 0.0, NEG_INF)
                for h in range(per_block):
                    s[per_block * n + h] = s[per_block * n + h] + unselected
        for h in range(per_block):
            s[per_block * qb + h] = s[per_block * qb + h] + causal[h]
        m = _reduce_rows(s, jnp.maximum, jnp.max)
        p = jnp.concatenate([jnp.exp2(x - m).astype(BF16) for x in s], axis=0)
        acc = _dot(vt_ref[a, :, 0:(qb + 1) * blk], p)
        return acc[:HEAD_DIM] * (1.0 / acc[HEAD_DIM:HEAD_DIM + 1])

    units = [(qb, hp) for qb in range(n_blocks) for hp in range(n_pairs)]
    _skewed(units, (load_q, scores, softmax_pv, store), (0, 1, 2, 3))


def _moba_call(qkv, bsz):
    t, n = qkv.shape
    seq = t // bsz
    d = n // 3
    cw = MOBA_HEAD_PAIRS * LANES
    col_blocks = d // cw
    return pl.pallas_call(
        _moba_kernel,
        grid=(bsz, col_blocks),
        in_specs=[pl.BlockSpec((seq, cw), lambda b, c: (b, c)),
                  pl.BlockSpec((seq, cw), lambda b, c: (b, col_blocks + c)),
                  pl.BlockSpec((seq, cw), lambda b, c: (b, 2 * col_blocks + c))],
        out_specs=pl.BlockSpec((seq, cw), lambda b, c: (b, c)),
        out_shape=jax.ShapeDtypeStruct((t, d), BF16),
        scratch_shapes=[pltpu.VMEM((2 * MOBA_HEAD_PAIRS, HEAD_DIM + ONES_ROWS, seq), BF16)],
        compiler_params=_params(("arbitrary", "arbitrary"), 40),
        name="moba",
    )(qkv, qkv, qkv)


def _band_kernel(q_ref, k_ref, v_ref, o_ref, lse_ref, *, blocks_per_seq):
    w = BAND
    n_blocks = q_ref.shape[0] // w
    n_pairs = q_ref.shape[1] // LANES
    key_i = lax.broadcasted_iota(jnp.int32, (w, 2 * w), 0)
    qry_i = lax.broadcasted_iota(jnp.int32, (w, 2 * w), 1) % w
    bias_of = {0: jnp.where(key_i <= qry_i, 0.0, NEG_INF), 1: jnp.where(key_i >= qry_i, 0.0, NEG_INF)}
    dim = lax.broadcasted_iota(jnp.int32, (LANES, w), 0)
    rows = lambda j: slice(j * w, (j + 1) * w)
    cols = lambda hp: slice(hp * LANES, (hp + 1) * LANES)

    def back_blocks(j):
        return (0,) if j % blocks_per_seq == 0 else (0, 1)

    v_t = {}

    def transposes(unit, _):
        j, hp = unit
        v_t[unit] = v_ref[rows(j), cols(hp)].astype(F32).T.astype(BF16)
        q_t = q_ref[rows(j), cols(hp)].astype(F32).T
        return jnp.concatenate([jnp.where(_in_head(dim, a), q_t, 0.0) for a in range(2)],
                               axis=1).astype(BF16)

    def scores(unit, q2):
        j, hp = unit
        return [_dot(k_ref[rows(j - back), cols(hp)], q2) for back in back_blocks(j)]

    def softmax_pv(unit, s_blocks):
        j, hp = unit
        s_blocks = [s + bias_of[back] for s, back in zip(s_blocks, back_blocks(j))]
        m = _reduce_rows(s_blocks, jnp.maximum, jnp.max)
        p_blocks = [jnp.exp2(s - m) for s in s_blocks]
        l = _reduce_rows(p_blocks, jnp.add, jnp.sum)
        o_t = functools.reduce(jnp.add, [_dot(v_t[j - back, hp], p.astype(BF16))
                                         for p, back in zip(p_blocks, back_blocks(j))]) * (1.0 / l)
        if j % blocks_per_seq != 0:
            del v_t[j - 1, hp]
        if (j + 1) % blocks_per_seq == 0:
            del v_t[unit]
        lse = (m + jnp.log2(l)) * LN2
        lse_ref[0, 2 * hp:2 * hp + 1, rows(j)] = lse[:, :w]
        lse_ref[0, 2 * hp + 1:2 * hp + 2, rows(j)] = lse[:, w:]
        return jnp.concatenate([o_t[:HEAD_DIM, :w], o_t[HEAD_DIM:, w:]], axis=0)

    def store(unit, o_sel):
        j, hp = unit
        o_ref[rows(j), cols(hp)] = o_sel.T.astype(BF16)

    units = [(j, hp) for j in range(n_blocks) for hp in range(n_pairs)]
    _skewed(units, (transposes, scores, softmax_pv, store), (0, 3, 6, 9))


def _band_call(q, k, v, dil, bsz):
    t, d = q.shape
    seq = t // bsz
    cw = BAND_HEAD_PAIRS * LANES
    blocks_per_seq = (seq // dil) // BAND
    blk = pl.BlockSpec((seq, cw), lambda b, c: (b, c))
    return pl.pallas_call(
        functools.partial(_band_kernel, blocks_per_seq=blocks_per_seq),
        grid=(bsz, d // cw),
        in_specs=[blk, blk, blk],
        out_specs=[blk, pl.BlockSpec((1, 2 * BAND_HEAD_PAIRS, seq), lambda b, c: (b, c, 0))],
        out_shape=[jax.ShapeDtypeStruct((t, d), BF16),
                   jax.ShapeDtypeStruct((bsz, N_HEADS, seq), F32)],
        compiler_params=_params(("arbitrary", "arbitrary"), 40),
        name="band%d" % dil,
    )(q, k, v)


def _lse_token_order(lse, dil):
    bsz, n_h, seq = lse.shape
    return lse.reshape(bsz, n_h, dil, seq // dil).transpose(0, 3, 2, 1).reshape(bsz * seq, n_h)


def _rope_tables(seq):
    inv = 1.0 / (ROPE_THETA ** (jnp.arange(0, HEAD_DIM, 2, dtype=F32) / HEAD_DIM))
    ang = jnp.arange(seq, dtype=F32)[:, None] * inv[None, :]
    cos, sin = jnp.cos(ang), jnp.sin(ang)
    tabs = (jnp.concatenate([cos, cos, cos, cos], axis=1),
            jnp.concatenate([-sin, -sin, sin, sin], axis=1))

    def strided(tab, dil):
        tiles = seq // TOKEN_TILE
        return tab.reshape(tiles, TOKEN_TILE // dil, dil, LANES).transpose(0, 2, 1, 3).reshape(seq, LANES)

    return tuple(jnp.stack([strided(tab, dil) for dil in DILATIONS]) for tab in tabs)


def _pair_split_columns(w, sections):
    d = w.shape[0]
    half = HEAD_DIM // 2
    cols = []
    for s, (rope, _, _) in enumerate(sections):
        ws = w[:, s * D_MODEL:(s + 1) * D_MODEL]
        if rope:
            ws = ws.reshape(d, D_MODEL // LANES, 2, 2, half).transpose(0, 1, 3, 2, 4).reshape(d, D_MODEL)
        cols.append(ws)
    return jnp.concatenate(cols, axis=1).astype(BF16)


def _vec(bsz, *rows):
    rows = [jnp.broadcast_to(r, (bsz, D_MODEL)) for r in rows]
    rows += [jnp.zeros((bsz, D_MODEL), F32)] * (8 - len(rows))
    return jnp.stack(rows, axis=1)


def kernel(x, c, ada_w, ada_b, norm_g, ffn_w_gate, ffn_w_up, ffn_w_down, moba_w_qkv, moba_w_o,
           kv_ada_w, kv_ada_b, kv_norm_g, kv_w, dil_w_q, dil_w_o, final_g):
    bsz, seq, d = x.shape
    t = bsz * seq
    tables = _rope_tables(seq)
    natural_tables = tuple(tab[:1] for tab in tables)
    mod = _mod_call(c, ada_w, ada_b).reshape(DEPTH, bsz, 3, 3, d)
    kv_mod = _mod_call(c, kv_ada_w[None], kv_ada_b[None]).reshape(bsz, 2, d)
    head_expand = jnp.tile(jnp.repeat(jnp.eye(N_HEADS, dtype=BF16), HEAD_DIM, axis=1), (2, 1))
    q_scale = HEAD_DIM ** -0.5 * LOG2E

    xf = x.reshape(t, d)
    kvs = None
    for layer in range(DEPTH):
        m = mod[layer]
        if layer == N_A_LAYERS:
            kv_sections = tuple((is_k, 1.0, dil) for dil in DILATIONS for is_k in (True, False))
            kvs = _proj_call(xf, _vec(bsz, kv_norm_g, kv_mod[:, 0], kv_mod[:, 1]), tables,
                             _pair_split_columns(kv_w, kv_sections), sections=kv_sections,
                             fused_out=False, bsz=bsz)
        ffn_w = [w[layer, 0].astype(BF16) for w in (ffn_w_gate, ffn_w_up, ffn_w_down)]
        xf = _ffn_call(xf, _vec(bsz, norm_g[layer, 0], m[:, 0, 0], m[:, 0, 1], m[:, 0, 2]), *ffn_w)

        mix_vec = _vec(bsz, norm_g[layer, 1], m[:, 1, 0], m[:, 1, 1])
        if layer < N_A_LAYERS:
            qkv_sections = ((True, q_scale, 1), (True, 1.0, 1), (False, 1.0, 1))
            qkv, = _proj_call(xf, mix_vec, natural_tables,
                              _pair_split_columns(moba_w_qkv[layer], qkv_sections),
                              sections=qkv_sections, fused_out=True, bsz=bsz)
            mixer = (_moba_call(qkv, bsz), moba_w_o[layer].astype(BF16))
            mode = "moba"
        else:
            lb = layer - N_A_LAYERS
            q_sections = tuple((True, q_scale, dil) for dil in DILATIONS)
            qs = _proj_call(xf, mix_vec, tables, _pair_split_columns(dil_w_q[lb], q_sections),
                            sections=q_sections, fused_out=False, bsz=bsz)
            outs, lses = [], []
            for g, (window, dil) in enumerate(DILATED_BRANCHES):
                assert window // dil == BAND
                o_g, lse_g = _band_call(qs[g].reshape(t, d), kvs[2 * g].reshape(t, d),
                                        kvs[2 * g + 1].reshape(t, d), dil, bsz)
                outs.append(o_g if dil == 1 else o_g.reshape(bsz, dil, seq // dil, d))
                lses.append(_lse_token_order(lse_g, dil))
            mixer = (*outs, *lses, head_expand, dil_w_o[lb].astype(BF16))
            mode = "dil"
        ffn_w = [w[layer, 1].astype(BF16) for w in (ffn_w_gate, ffn_w_up, ffn_w_down)]
        xf = _ffn_call(xf, _vec(bsz, norm_g[layer, 2], m[:, 2, 0], m[:, 2, 1], m[:, 2, 2],
                                m[:, 1, 2], final_g),
                       *ffn_w, mode=mode, final=(layer == DEPTH - 1), mixer=mixer)
    return xf.reshape(bsz, seq, d)
```

```python
import functools
import math

import jax
import jax.numpy as jnp
from jax import lax
from jax.experimental import pallas as pl
from jax.experimental.pallas import tpu as pltpu

F32 = jnp.float32
BF16 = jnp.bfloat16

D_MODEL = 1024
HEAD_DIM = 64
N_HEADS = D_MODEL // HEAD_DIM
D_FF = 2816
ROPE_THETA = 10000.0
RMS_EPS = 1e-6
DEPTH = 4
N_A_LAYERS = 2
MOBA_BLOCK = 256
MOBA_TOPK = 3
MOBA_PIECE = 128
ONES_ROWS = 16
MOBA_HEAD_PAIRS = 2
DILATED_BRANCHES = ((128, 1), (512, 4), (2048, 16))
N_BRANCHES = len(DILATED_BRANCHES)
DILATIONS = tuple(d for _, d in DILATED_BRANCHES)
BAND = 128
LANES = 128
FFN_CHUNK = 256
TOKEN_TILE = 512
BAND_HEAD_PAIRS = 4
NEG_INF = float("-inf")
LOG2E = math.log2(math.e)
LN2 = math.log(2.0)
MIB = 1024 * 1024


def _dot(a, b):
    return jnp.dot(a, b, preferred_element_type=F32)


def _split_bf16(a):
    hi = a.astype(BF16)
    lo = (a - hi.astype(F32)).astype(BF16)
    return hi, lo


def _resident(shape, lead=()):
    return pl.BlockSpec((None,) * len(lead) + tuple(shape), lambda *_: tuple(lead) + (0,) * len(shape),
                        pipeline_mode=pl.Buffered(1))


def _fold_rows(x, op):
    while x.shape[0] > 8 and x.shape[0] % 16 == 0:
        half = x.shape[0] // 2
        x = op(x[:half], x[half:])
    return x


def _reduce_rows(pieces, op, reduce_fn):
    folded = functools.reduce(op, [_fold_rows(x, op) for x in pieces])
    return reduce_fn(folded, axis=0, keepdims=True)


def _params(semantics, vmem_mib):
    return pltpu.CompilerParams(dimension_semantics=semantics,
                                vmem_limit_bytes=vmem_mib * MIB)


def _skewed(units, stages, delays):
    results = {}
    for t in range(len(units) + delays[-1]):
        for k, (stage, delay) in enumerate(zip(stages, delays)):
            i = t - delay
            if 0 <= i < len(units):
                results[k, i] = stage(units[i], results.pop((k - 1, i), None))


def _mod_kernel(c_ref, w_ref, b_ref, o_ref):
    c = c_ref[...]
    a_hi, a_lo = _split_bf16(c * jax.nn.sigmoid(c))
    w_hi, w_lo = _split_bf16(w_ref[0])
    n_b = c.shape[0]
    both = _dot(jnp.concatenate([a_hi, a_lo], axis=0), w_hi)
    o_ref[0] = both[:n_b] + both[n_b:] + _dot(a_hi, w_lo) + b_ref[0]


def _mod_call(c, w, b):
    n_l, d, n = w.shape
    bsz = c.shape[0]
    tn = 1024
    return pl.pallas_call(
        _mod_kernel,
        grid=(n_l, n // tn),
        in_specs=[pl.BlockSpec((bsz, d), lambda l, j: (0, 0)),
                  pl.BlockSpec((1, d, tn), lambda l, j: (l, 0, j)),
                  pl.BlockSpec((1, 1, tn), lambda l, j: (l, 0, j))],
        out_specs=pl.BlockSpec((1, bsz, tn), lambda l, j: (l, 0, j)),
        out_shape=jax.ShapeDtypeStruct((n_l, bsz, n), F32),
        compiler_params=_params(("arbitrary", "arbitrary"), 40),
        name="mod",
    )(c, w, b.reshape(n_l, 1, n))


def _norm_mod(x, g, shift, scale):
    ms = jnp.mean(x * x, axis=-1, keepdims=True)
    return (x * lax.rsqrt(ms + RMS_EPS) * g) * (1.0 + scale) + shift


def _store_lane_chunks(ref3, x):
    for c in range(ref3.shape[0]):
        ref3[c] = x[:, c * LANES:(c + 1) * LANES]


def _load_lane_chunks(ref3):
    return jnp.concatenate([ref3[c] for c in range(ref3.shape[0])], axis=1)


def _load_strided_order(ref3, dil):
    rows = ref3.shape[1] // dil
    cols = [jnp.concatenate([ref3[c, pl.ds(r, rows, stride=dil), :] for r in range(dil)], axis=0)
            for c in range(ref3.shape[0])]
    return jnp.concatenate(cols, axis=1)


def _store_token_order(ref3, block_ref, dil):
    rows = block_ref.shape[2]
    for r in range(dil):
        piece = block_ref[0, r].astype(F32)
        for c in range(ref3.shape[0]):
            ref3[c, pl.ds(r, rows, stride=dil), :] = piece[:, c * LANES:(c + 1) * LANES]


def _ffn_kernel(*refs, mode, final):
    it = iter(refs)
    x_ref, vec_ref = next(it), next(it)
    x = x_ref[...]
    vec = vec_ref[0]
    if mode == "moba":
        attn_ref, wo_ref = next(it), next(it)
        x = x + vec[4:5] * _dot(attn_ref[...], wo_ref[...])
    elif mode == "dil":
        o_refs = [next(it) for _ in range(N_BRANCHES)]
        l_refs = [next(it) for _ in range(N_BRANCHES)]
        e_ref, wo_ref = next(it), next(it)
    wg_ref, wu_ref, wd_ref = next(it), next(it), next(it)
    o_ref, a_ref = next(it), next(it)
    if mode == "dil":
        u_ref = next(it)
        lse = [r[...] for r in l_refs]
        mx = functools.reduce(jnp.maximum, lse)
        ex = [jnp.exp(l - mx) for l in lse]
        inv = 1.0 / functools.reduce(lambda a, b: a + b, ex)
        attn = None
        for e, br_ref, dil in zip(ex, o_refs, DILATIONS):
            if dil == 1:
                o_g = br_ref[...].astype(F32)
            else:
                _store_token_order(u_ref, br_ref, dil)
                o_g = _load_lane_chunks(u_ref)
            w_full = _dot(jnp.concatenate(_split_bf16(e * inv), axis=1), e_ref[...])
            term = w_full * o_g
            attn = term if attn is None else attn + term
        x = x + vec[4:5] * _dot(attn.astype(BF16), wo_ref[...])

    h = _norm_mod(x, vec[0:1], vec[1:2], vec[2:3]).astype(BF16)
    for c in range(D_FF // FFN_CHUNK):
        sl = slice(c * FFN_CHUNK, (c + 1) * FFN_CHUNK)
        gate = _dot(h, wg_ref[:, sl])
        up = _dot(h, wu_ref[:, sl])
        a_ref[:, sl] = (gate * jax.nn.sigmoid(gate) * up).astype(BF16)
    y = x + (0.5 * vec[3:4]) * _dot(a_ref[...], wd_ref[...])
    if final:
        ms = jnp.mean(y * y, axis=-1, keepdims=True)
        y = y * lax.rsqrt(ms + RMS_EPS) * vec[5:6]
    o_ref[...] = y


def _strided_block_spec(dil, tm, tiles_per_seq):
    return pl.BlockSpec((1, dil, tm // dil, D_MODEL),
                        lambda i: (i // tiles_per_seq, 0, i % tiles_per_seq, 0))


def _ffn_call(x, vec, ffn_w, layer, half, *, mode="none", final=False, mixer=(), mixer_layer=0):
    t, d = x.shape
    bsz = vec.shape[0]
    tm = TOKEN_TILE
    tiles_per_seq = (t // bsz) // tm
    row = lambda i: (i, 0)
    in_specs = [pl.BlockSpec((tm, d), row),
                pl.BlockSpec((1, 8, d), lambda i: (i // tiles_per_seq, 0, 0))]
    scratch = [pltpu.VMEM((tm, D_FF), BF16)]
    if mode == "moba":
        in_specs += [pl.BlockSpec((tm, d), row), _resident((d, d), (mixer_layer,))]
    elif mode == "dil":
        for dil in DILATIONS:
            in_specs.append(pl.BlockSpec((tm, d), row) if dil == 1
                            else _strided_block_spec(dil, tm, tiles_per_seq))
        in_specs += [pl.BlockSpec((tm, N_HEADS), row)] * N_BRANCHES
        in_specs += [_resident((2 * N_HEADS, d)), _resident((d, d), (mixer_layer,))]
        scratch.append(pltpu.VMEM((d // LANES, tm, LANES), F32))
    which = (layer, half)
    in_specs += [_resident((d, D_FF), which), _resident((d, D_FF), which), _resident((D_FF, d), which)]
    return pl.pallas_call(
        functools.partial(_ffn_kernel, mode=mode, final=final),
        grid=(t // tm,),
        in_specs=in_specs,
        out_specs=pl.BlockSpec((tm, d), row),
        out_shape=jax.ShapeDtypeStruct((t, d), F32),
        scratch_shapes=scratch,
        compiler_params=_params(("arbitrary",), 52),
        name="ffn_" + mode,
    )(x, vec, *mixer, *ffn_w)


def _in_head(dim, a):
    return (dim // (HEAD_DIM // 2)) % 2 == a


def _rope(y, cos, sin):
    pieces = []
    for c in range(y.shape[1] // LANES):
        yc = y[:, c * LANES:(c + 1) * LANES]
        pieces.append(yc * cos + pltpu.roll(yc, LANES // 2, 1) * sin)
    return jnp.concatenate(pieces, axis=1)


def _proj_kernel(x_ref, vec_ref, cos_ref, sin_ref, w_ref, *rest, sections, fused_out):
    o_refs, h_ref = rest[:-1], rest[-1]
    tm = x_ref.shape[0]
    vec = vec_ref[0]
    h = _norm_mod(x_ref[...], vec[0:1], vec[1:2], vec[2:3])
    dils = sorted({dil for _, _, dil in sections})
    if dils != [1]:
        _store_lane_chunks(h_ref, h)
    h_by_dil = {dil: (h if dil == 1 else _load_strided_order(h_ref, dil)).astype(BF16) for dil in dils}
    for s, (rope, scale, dil) in enumerate(sections):
        y = _dot(h_by_dil[dil], w_ref[:, s * D_MODEL:(s + 1) * D_MODEL])
        if rope:
            t_i = dils.index(dil)
            y = _rope(y, cos_ref[t_i], sin_ref[t_i])
        if scale != 1.0:
            y = y * scale
        y = y.astype(BF16)
        if fused_out:
            o_refs[0][:, s * D_MODEL:(s + 1) * D_MODEL] = y
        elif dil == 1:
            o_refs[s][...] = y
        else:
            rows = tm // dil
            for r in range(dil):
                o_refs[s][0, r] = y[r * rows:(r + 1) * rows, :]


def _proj_call(x, vec, tables, w, layer, *, sections, fused_out, bsz):
    t, d = x.shape
    n = w.shape[2]
    tm = TOKEN_TILE
    seq = t // bsz
    tiles_per_seq = seq // tm
    row = lambda i: (i, 0)
    n_tab = tables[0].shape[0]
    tab_spec = pl.BlockSpec((n_tab, tm, LANES), lambda i: (0, i % tiles_per_seq, 0))
    if fused_out:
        out_specs = [pl.BlockSpec((tm, n), row)]
        out_shape = [jax.ShapeDtypeStruct((t, n), BF16)]
    else:
        out_specs, out_shape = [], []
        for _, _, dil in sections:
            if dil == 1:
                out_specs.append(pl.BlockSpec((tm, d), row))
                out_shape.append(jax.ShapeDtypeStruct((t, d), BF16))
            else:
                out_specs.append(_strided_block_spec(dil, tm, tiles_per_seq))
                out_shape.append(jax.ShapeDtypeStruct((bsz, dil, seq // dil, d), BF16))
    return pl.pallas_call(
        functools.partial(_proj_kernel, sections=sections, fused_out=fused_out),
        grid=(t // tm,),
        in_specs=[pl.BlockSpec((tm, d), row),
                  pl.BlockSpec((1, 8, d), lambda i: (i // tiles_per_seq, 0, 0)),
                  tab_spec, tab_spec,
                  _resident((d, n), (layer,))],
        out_specs=out_specs,
        out_shape=out_shape,
        scratch_shapes=[pltpu.VMEM((d // LANES, tm, LANES), F32)],
        compiler_params=_params(("arbitrary",), 48),
        name="proj%d" % n,
    )(x, vec, *tables, w)


def _moba_kernel(q_ref, k_ref, v_ref, o_ref, vt_ref):
    blk = MOBA_BLOCK
    n_blocks = q_ref.shape[0] // blk
    n_pairs = q_ref.shape[1] // LANES
    rows = lambda n: slice(n * blk, (n + 1) * blk)
    cols = lambda hp: slice(hp * LANES, (hp + 1) * LANES)

    k_mean = []
    for hp in range(n_pairs):
        block_means = []
        for a in range(2):
            vt_ref[2 * hp + a, HEAD_DIM:, :] = jnp.ones((ONES_ROWS, vt_ref.shape[2]), BF16)
        for n in range(n_blocks):
            v_t = v_ref[rows(n), cols(hp)].astype(F32).T.astype(BF16)
            for a in range(2):
                vt_ref[2 * hp + a, :HEAD_DIM, rows(n)] = v_t[a * HEAD_DIM:(a + 1) * HEAD_DIM, :]
            block_means.append(jnp.mean(k_ref[rows(n), cols(hp)].astype(F32), axis=0, keepdims=True))
        k_mean.append(_split_bf16(jnp.concatenate(block_means, axis=0)))

    dim = lax.broadcasted_iota(jnp.int32, (LANES, blk), 0)
    in_head = [_in_head(dim, a) for a in range(2)]
    cand = lax.broadcasted_iota(jnp.int32, (n_blocks, blk), 0)
    piece = MOBA_PIECE
    per_block = blk // piece
    key_i = lax.broadcasted_iota(jnp.int32, (piece, blk), 0)
    qry_i = lax.broadcasted_iota(jnp.int32, (piece, blk), 1)
    causal = [jnp.where(key_i + h * piece <= qry_i, 0.0, NEG_INF) for h in range(per_block)]

    def load_q(unit, _):
        qb, hp = unit
        q_t = q_ref[rows(qb), cols(hp)].astype(F32).T
        return [jnp.where(in_head[a], q_t, 0.0).astype(BF16) for a in range(2)]

    def scores(unit, q_heads):
        qb, hp = unit
        km_hi, km_lo = k_mean[hp]
        out = []
        for q_a in q_heads:
            s_all = _dot(k_ref[0:(qb + 1) * blk, cols(hp)], q_a)
            s = [s_all[i * piece:(i + 1) * piece, :] for i in range(per_block * (qb + 1))]
            gate = _dot(km_hi, q_a) + _dot(km_lo, q_a) if qb > MOBA_TOPK else None
            out.append((s, gate))
        return out

    def softmax_pv(unit, stage):
        qb, hp = unit
        return [head_softmax_pv(qb, 2 * hp + a, *stage[a]) for a in range(2)]

    def store(unit, heads):
        qb, hp = unit
        o_ref[rows(qb), cols(hp)] = jnp.concatenate(heads, axis=0).T.astype(BF16)

    def head_softmax_pv(qb, a, s, gate):
        s = list(s)
        if gate is not None:
            for n in range(qb):
                g_n = gate[n:n + 1, :]
                beats = ((gate > g_n) | ((gate == g_n) & (cand < n))) & (cand < qb)
                rank = jnp.sum(jnp.where(beats, 1.0, 0.0), axis=0, keepdims=True)
                unselected = jnp.where(rank < MOBA_TOPK, 0.0, NEG_INF)
                for h in range(per_block):
                    s[per_block * n + h] = s[per_block * n + h] + unselected
        for h in range(per_block):
            s[per_block * qb + h] = s[per_block * qb + h] + causal[h]
        m = _reduce_rows(s, jnp.maximum, jnp.max)
        p = jnp.concatenate([jnp.exp2(x - m).astype(BF16) for x in s], axis=0)
        acc = _dot(vt_ref[a, :, 0:(qb + 1) * blk], p)
        return acc[:HEAD_DIM] * (1.0 / acc[HEAD_DIM:HEAD_DIM + 1])

    units = [(qb, hp) for qb in range(n_blocks) for hp in range(n_pairs)]
    _skewed(units, (load_q, scores, softmax_pv, store), (0, 1, 2, 3))


def _moba_call(qkv, bsz):
    t, n = qkv.shape
    seq = t // bsz
    d = n // 3
    cw = MOBA_HEAD_PAIRS * LANES
    col_blocks = d // cw
    return pl.pallas_call(
        _moba_kernel,
        grid=(bsz, col_blocks),
        in_specs=[pl.BlockSpec((seq, cw), lambda b, c: (b, c)),
                  pl.BlockSpec((seq, cw), lambda b, c: (b, col_blocks + c)),
                  pl.BlockSpec((seq, cw), lambda b, c: (b, 2 * col_blocks + c))],
        out_specs=pl.BlockSpec((seq, cw), lambda b, c: (b, c)),
        out_shape=jax.ShapeDtypeStruct((t, d), BF16),
        scratch_shapes=[pltpu.VMEM((2 * MOBA_HEAD_PAIRS, HEAD_DIM + ONES_ROWS, seq), BF16)],
        compiler_params=_params(("arbitrary", "arbitrary"), 40),
        name="moba",
    )(qkv, qkv, qkv)


def _band_kernel(q_ref, k_ref, v_ref, o_ref, lse_ref, *, blocks_per_seq):
    w = BAND
    n_blocks = q_ref.shape[0] // w
    n_pairs = q_ref.shape[1] // LANES
    key_i = lax.broadcasted_iota(jnp.int32, (w, 2 * w), 0)
    qry_i = lax.broadcasted_iota(jnp.int32, (w, 2 * w), 1) % w
    bias_of = {0: jnp.where(key_i <= qry_i, 0.0, NEG_INF), 1: jnp.where(key_i >= qry_i, 0.0, NEG_INF)}
    dim = lax.broadcasted_iota(jnp.int32, (LANES, w), 0)
    rows = lambda j: slice(j * w, (j + 1) * w)
    cols = lambda hp: slice(hp * LANES, (hp + 1) * LANES)

    def back_blocks(j):
        return (0,) if j % blocks_per_seq == 0 else (0, 1)

    v_t = {}

    def transposes(unit, _):
        j, hp = unit
        v_t[unit] = v_ref[rows(j), cols(hp)].astype(F32).T.astype(BF16)
        q_t = q_ref[rows(j), cols(hp)].astype(F32).T
        return jnp.concatenate([jnp.where(_in_head(dim, a), q_t, 0.0) for a in range(2)],
                               axis=1).astype(BF16)

    def scores(unit, q2):
        j, hp = unit
        return [_dot(k_ref[rows(j - back), cols(hp)], q2) for back in back_blocks(j)]

    def softmax_pv(unit, s_blocks):
        j, hp = unit
        s_blocks = [s + bias_of[back] for s, back in zip(s_blocks, back_blocks(j))]
        m = _reduce_rows(s_blocks, jnp.maximum, jnp.max)
        p_blocks = [jnp.exp2(s - m) for s in s_blocks]
        l = _reduce_rows(p_blocks, jnp.add, jnp.sum)
        o_t = functools.reduce(jnp.add, [_dot(v_t[j - back, hp], p.astype(BF16))
                                         for p, back in zip(p_blocks, back_blocks(j))]) * (1.0 / l)
        if j % blocks_per_seq != 0:
            del v_t[j - 1, hp]
        if (j + 1) % blocks_per_seq == 0:
            del v_t[unit]
        lse = (m + jnp.log2(l)) * LN2
        lse_ref[0, 2 * hp:2 * hp + 1, rows(j)] = lse[:, :w]
        lse_ref[0, 2 * hp + 1:2 * hp + 2, rows(j)] = lse[:, w:]
        return jnp.concatenate([o_t[:HEAD_DIM, :w], o_t[HEAD_DIM:, w:]], axis=0)

    def store(unit, o_sel):
        j, hp = unit
        o_ref[rows(j), cols(hp)] = o_sel.T.astype(BF16)

    units = [(j, hp) for j in range(n_blocks) for hp in range(n_pairs)]
    _skewed(units, (transposes, scores, softmax_pv, store), (0, 3, 6, 9))


def _band_call(q, k, v, dil, bsz):
    t, d = q.shape
    seq = t // bsz
    cw = BAND_HEAD_PAIRS * LANES
    blocks_per_seq = (seq // dil) // BAND
    blk = pl.BlockSpec((seq, cw), lambda b, c: (b, c))
    return pl.pallas_call(
        functools.partial(_band_kernel, blocks_per_seq=blocks_per_seq),
        grid=(bsz, d // cw),
        in_specs=[blk, blk, blk],
        out_specs=[blk, pl.BlockSpec((1, 2 * BAND_HEAD_PAIRS, seq), lambda b, c: (b, c, 0))],
        out_shape=[jax.ShapeDtypeStruct((t, d), BF16),
                   jax.ShapeDtypeStruct((bsz, N_HEADS, seq), F32)],
        compiler_params=_params(("arbitrary", "arbitrary"), 40),
        name="band%d" % dil,
    )(q, k, v)


def _lse_token_order(lse, dil):
    bsz, n_h, seq = lse.shape
    return lse.reshape(bsz, n_h, dil, seq // dil).transpose(0, 3, 2, 1).reshape(bsz * seq, n_h)


def _rope_tables(seq):
    inv = 1.0 / (ROPE_THETA ** (jnp.arange(0, HEAD_DIM, 2, dtype=F32) / HEAD_DIM))
    ang = jnp.arange(seq, dtype=F32)[:, None] * inv[None, :]
    cos, sin = jnp.cos(ang), jnp.sin(ang)
    tabs = (jnp.concatenate([cos, cos, cos, cos], axis=1),
            jnp.concatenate([-sin, -sin, sin, sin], axis=1))

    def strided(tab, dil):
        tiles = seq // TOKEN_TILE
        return tab.reshape(tiles, TOKEN_TILE // dil, dil, LANES).transpose(0, 2, 1, 3).reshape(seq, LANES)

    return tuple(jnp.stack([strided(tab, dil) for dil in DILATIONS]) for tab in tabs)


def _prep_kernel(w_ref, perm_ref, o_ref, *, rope):
    for s, roped in enumerate(rope):
        for c in range(D_MODEL // LANES):
            sl = slice(s * D_MODEL + c * LANES, s * D_MODEL + (c + 1) * LANES)
            chunk = w_ref[:, sl].astype(BF16)
            o_ref[:, sl] = _dot(chunk, perm_ref[...]).astype(BF16) if roped else chunk


def _pair_split_matrix():
    half = HEAD_DIM // 2
    dst = jnp.arange(LANES)
    group, r = dst // half, dst % half
    src = (group % 2) * HEAD_DIM + (group // 2) * half + r
    return (jnp.arange(LANES)[:, None] == src[None, :]).astype(BF16)


def _prep_call(w, sections):
    n_l, d, n = w.shape
    rows = 256
    blk = pl.BlockSpec((None, rows, n), lambda l, i: (l, i, 0))
    return pl.pallas_call(
        functools.partial(_prep_kernel, rope=tuple(rope for rope, _, _ in sections)),
        grid=(n_l, d // rows),
        in_specs=[blk, pl.BlockSpec((LANES, LANES), lambda l, i: (0, 0))],
        out_specs=blk,
        out_shape=jax.ShapeDtypeStruct((n_l, d, n), BF16),
        compiler_params=_params(("arbitrary", "arbitrary"), 40),
        name="prep%d" % n,
    )(w, _pair_split_matrix())


def _vec(bsz, *rows):
    rows = [jnp.broadcast_to(r, (bsz, D_MODEL)) for r in rows]
    rows += [jnp.zeros((bsz, D_MODEL), F32)] * (8 - len(rows))
    return jnp.stack(rows, axis=1)


def kernel(x, c, ada_w, ada_b, norm_g, ffn_w_gate, ffn_w_up, ffn_w_down, moba_w_qkv, moba_w_o,
           kv_ada_w, kv_ada_b, kv_norm_g, kv_w, dil_w_q, dil_w_o, final_g):
    bsz, seq, d = x.shape
    t = bsz * seq
    tables = _rope_tables(seq)
    natural_tables = tuple(tab[:1] for tab in tables)
    mod = _mod_call(c, ada_w, ada_b).reshape(DEPTH, bsz, 3, 3, d)
    kv_mod = _mod_call(c, kv_ada_w[None], kv_ada_b[None]).reshape(bsz, 2, d)
    head_expand = jnp.tile(jnp.repeat(jnp.eye(N_HEADS, dtype=BF16), HEAD_DIM, axis=1), (2, 1))
    q_scale = HEAD_DIM ** -0.5 * LOG2E

    ffn_w = tuple(w.astype(BF16) for w in (ffn_w_gate, ffn_w_up, ffn_w_down))
    moba_wo, dil_wo = moba_w_o.astype(BF16), dil_w_o.astype(BF16)
    qkv_sections = ((True, q_scale, 1), (True, 1.0, 1), (False, 1.0, 1))
    q_sections = tuple((True, q_scale, dil) for dil in DILATIONS)
    kv_sections = tuple((is_k, 1.0, dil) for dil in DILATIONS for is_k in (True, False))
    moba_wqkv = _prep_call(moba_w_qkv, qkv_sections)
    dil_wq = _prep_call(dil_w_q, q_sections)
    kv_wp = _prep_call(kv_w[None], kv_sections)

    xf = x.reshape(t, d)
    kvs = None
    for layer in range(DEPTH):
        m = mod[layer]
        if layer == N_A_LAYERS:
            kvs = _proj_call(xf, _vec(bsz, kv_norm_g, kv_mod[:, 0], kv_mod[:, 1]), tables, kv_wp, 0,
                             sections=kv_sections, fused_out=False, bsz=bsz)
        xf = _ffn_call(xf, _vec(bsz, norm_g[layer, 0], m[:, 0, 0], m[:, 0, 1], m[:, 0, 2]),
                       ffn_w, layer, 0)

        mix_vec = _vec(bsz, norm_g[layer, 1], m[:, 1, 0], m[:, 1, 1])
        if layer < N_A_LAYERS:
            lb = layer
            qkv, = _proj_call(xf, mix_vec, natural_tables, moba_wqkv, lb,
                              sections=qkv_sections, fused_out=True, bsz=bsz)
            mixer = (_moba_call(qkv, bsz), moba_wo)
            mode = "moba"
        else:
            lb = layer - N_A_LAYERS
            qs = _proj_call(xf, mix_vec, tables, dil_wq, lb,
                            sections=q_sections, fused_out=False, bsz=bsz)
            outs, lses = [], []
            for g, (window, dil) in enumerate(DILATED_BRANCHES):
                assert window // dil == BAND
                o_g, lse_g = _band_call(qs[g].reshape(t, d), kvs[2 * g].reshape(t, d),
                                        kvs[2 * g + 1].reshape(t, d), dil, bsz)
                outs.append(o_g if dil == 1 else o_g.reshape(bsz, dil, seq // dil, d))
                lses.append(_lse_token_order(lse_g, dil))
            mixer = (*outs, *lses, head_expand, dil_wo)
            mode = "dil"
        xf = _ffn_call(xf, _vec(bsz, norm_g[layer, 2], m[:, 2, 0], m[:, 2, 1], m[:, 2, 2],
                                m[:, 1, 2], final_g),
                       ffn_w, layer, 1, mode=mode, final=(layer == DEPTH - 1), mixer=mixer, mixer_layer=lb)
    return xf.reshape(bsz, seq, d)
```

```python
import functools
import math

import jax
import jax.numpy as jnp
from jax import lax
from jax.experimental import pallas as pl
from jax.experimental.pallas import tpu as pltpu

F32 = jnp.float32
BF16 = jnp.bfloat16

D_MODEL = 1024
HEAD_DIM = 64
N_HEADS = D_MODEL // HEAD_DIM
D_FF = 2816
ROPE_THETA = 10000.0
RMS_EPS = 1e-6
DEPTH = 4
N_A_LAYERS = 2
MOBA_BLOCK = 256
MOBA_TOPK = 3
MOBA_PIECE = 128
ONES_ROWS = 16
MOBA_HEAD_PAIRS = 2
DILATED_BRANCHES = ((128, 1), (512, 4), (2048, 16))
N_BRANCHES = len(DILATED_BRANCHES)
DILATIONS = tuple(d for _, d in DILATED_BRANCHES)
BAND = 128
LANES = 128
FFN_CHUNK = 256
TOKEN_TILE = 512
FFN_TILE = 1024
BAND_HEAD_PAIRS = 4
NEG_INF = float("-inf")
LOG2E = math.log2(math.e)
LN2 = math.log(2.0)
MIB = 1024 * 1024
V7X_VMEM_BYTES = 64 * MIB


def _dot(a, b):
    return jnp.dot(a, b, preferred_element_type=F32)


def _split_bf16(a):
    hi = a.astype(BF16)
    lo = (a - hi.astype(F32)).astype(BF16)
    return hi, lo


def _resident(shape, lead=()):
    return pl.BlockSpec((None,) * len(lead) + tuple(shape), lambda *_: tuple(lead) + (0,) * len(shape),
                        pipeline_mode=pl.Buffered(1))


def _fold_rows(x, op):
    while x.shape[0] > 8 and x.shape[0] % 16 == 0:
        half = x.shape[0] // 2
        x = op(x[:half], x[half:])
    return x


def _reduce_rows(pieces, op, reduce_fn):
    folded = functools.reduce(op, [_fold_rows(x, op) for x in pieces])
    return reduce_fn(folded, axis=0, keepdims=True)


def _params(semantics, *buffers):
    declared = sum(math.prod(shape) * jnp.dtype(dtype).itemsize * count for shape, dtype, count in buffers)
    limit = min(declared + V7X_VMEM_BYTES // 4, V7X_VMEM_BYTES - 8 * MIB)
    assert declared < limit, (declared, limit)
    return pltpu.CompilerParams(dimension_semantics=semantics, vmem_limit_bytes=limit)


def _skewed(units, stages, delays):
    results = {}
    for t in range(len(units) + delays[-1]):
        for k, (stage, delay) in enumerate(zip(stages, delays)):
            i = t - delay
            if 0 <= i < len(units):
                results[k, i] = stage(units[i], results.pop((k - 1, i), None))


def _mod_kernel(c_ref, w_ref, b_ref, o_ref):
    c = c_ref[...]
    a_hi, a_lo = _split_bf16(c * jax.nn.sigmoid(c))
    w_hi, w_lo = _split_bf16(w_ref[0])
    n_b = c.shape[0]
    both = _dot(jnp.concatenate([a_hi, a_lo], axis=0), w_hi)
    o_ref[0] = both[:n_b] + both[n_b:] + _dot(a_hi, w_lo) + b_ref[0]


def _mod_call(c, w, b):
    n_l, d, n = w.shape
    bsz = c.shape[0]
    tn = 1024
    return pl.pallas_call(
        _mod_kernel,
        grid=(n_l, n // tn),
        in_specs=[pl.BlockSpec((bsz, d), lambda l, j: (0, 0)),
                  pl.BlockSpec((1, d, tn), lambda l, j: (l, 0, j)),
                  pl.BlockSpec((1, 1, tn), lambda l, j: (l, 0, j))],
        out_specs=pl.BlockSpec((1, bsz, tn), lambda l, j: (l, 0, j)),
        out_shape=jax.ShapeDtypeStruct((n_l, bsz, n), F32),
        compiler_params=_params(("arbitrary", "arbitrary"), ((bsz, d), F32, 2), ((d, tn), F32, 2),
                                ((1, tn), F32, 2), ((bsz, tn), F32, 2)),
        name="mod",
    )(c, w, b.reshape(n_l, 1, n))


def _norm_mod(x, g, shift, scale):
    ms = jnp.mean(x * x, axis=-1, keepdims=True)
    return (x * lax.rsqrt(ms + RMS_EPS) * g) * (1.0 + scale) + shift


def _store_lane_chunks(ref3, x):
    for c in range(ref3.shape[0]):
        ref3[c] = x[:, c * LANES:(c + 1) * LANES]


def _load_lane_chunks(ref3):
    return jnp.concatenate([ref3[c] for c in range(ref3.shape[0])], axis=1)


def _load_strided_order(ref3, dil):
    rows = ref3.shape[1] // dil
    cols = [jnp.concatenate([ref3[c, pl.ds(r, rows, stride=dil), :] for r in range(dil)], axis=0)
            for c in range(ref3.shape[0])]
    return jnp.concatenate(cols, axis=1)


def _store_token_order(ref3, block_ref, dil):
    rows = block_ref.shape[2]
    for r in range(dil):
        piece = block_ref[0, r].astype(F32)
        for c in range(ref3.shape[0]):
            ref3[c, pl.ds(r, rows, stride=dil), :] = piece[:, c * LANES:(c + 1) * LANES]


def _ffn_kernel(*refs, mode, final):
    it = iter(refs)
    x_ref, vec_ref = next(it), next(it)
    x = x_ref[...]
    vec = vec_ref[0]
    if mode == "moba":
        attn_ref, wo_ref = next(it), next(it)
        x = x + vec[4:5] * _dot(attn_ref[...], wo_ref[...])
    elif mode == "dil":
        o_refs = [next(it) for _ in range(N_BRANCHES)]
        l_refs = [next(it) for _ in range(N_BRANCHES)]
        e_ref, wo_ref = next(it), next(it)
    wg_ref, wu_ref, wd_ref = next(it), next(it), next(it)
    o_ref, a_ref = next(it), next(it)
    if mode == "dil":
        u_ref = next(it)
        lse = [r[...] for r in l_refs]
        mx = functools.reduce(jnp.maximum, lse)
        ex = [jnp.exp(l - mx) for l in lse]
        inv = 1.0 / functools.reduce(lambda a, b: a + b, ex)
        attn = None
        for e, br_ref, dil in zip(ex, o_refs, DILATIONS):
            if dil == 1:
                o_g = br_ref[...].astype(F32)
            else:
                _store_token_order(u_ref, br_ref, dil)
                o_g = _load_lane_chunks(u_ref)
            w_full = _dot(jnp.concatenate(_split_bf16(e * inv), axis=1), e_ref[...])
            term = w_full * o_g
            attn = term if attn is None else attn + term
        x = x + vec[4:5] * _dot(attn.astype(BF16), wo_ref[...])

    h = _norm_mod(x, vec[0:1], vec[1:2], vec[2:3]).astype(BF16)
    for c in range(D_FF // FFN_CHUNK):
        sl = slice(c * FFN_CHUNK, (c + 1) * FFN_CHUNK)
        gate = _dot(h, wg_ref[:, sl])
        up = _dot(h, wu_ref[:, sl])
        a_ref[:, sl] = (gate * jax.nn.sigmoid(gate) * up).astype(BF16)
    y = x + (0.5 * vec[3:4]) * _dot(a_ref[...], wd_ref[...])
    if final:
        ms = jnp.mean(y * y, axis=-1, keepdims=True)
        y = y * lax.rsqrt(ms + RMS_EPS) * vec[5:6]
    o_ref[...] = y


def _strided_block_spec(dil, tm, tiles_per_seq):
    return pl.BlockSpec((1, dil, tm // dil, D_MODEL),
                        lambda i: (i // tiles_per_seq, 0, i % tiles_per_seq, 0))


def _ffn_call(x, vec, ffn_w, layer, half, *, mode="none", final=False, mixer=(), mixer_layer=0):
    t, d = x.shape
    bsz = vec.shape[0]
    tm = TOKEN_TILE if mode == "dil" else FFN_TILE
    tiles_per_seq = (t // bsz) // tm
    row = lambda i: (i, 0)
    in_specs = [pl.BlockSpec((tm, d), row),
                pl.BlockSpec((1, 8, d), lambda i: (i // tiles_per_seq, 0, 0))]
    scratch = [pltpu.VMEM((tm, D_FF), BF16)]
    buffers = [((tm, d), F32, 4), ((8, d), F32, 2), ((tm, D_FF), BF16, 1), ((d, D_FF), BF16, 3)]
    if mode == "moba":
        in_specs += [pl.BlockSpec((tm, d), row), _resident((d, d), (mixer_layer,))]
        buffers += [((tm, d), BF16, 2), ((d, d), BF16, 1)]
    elif mode == "dil":
        for dil in DILATIONS:
            in_specs.append(pl.BlockSpec((tm, d), row) if dil == 1
                            else _strided_block_spec(dil, tm, tiles_per_seq))
        in_specs += [pl.BlockSpec((tm, N_HEADS), row)] * N_BRANCHES
        in_specs += [_resident((2 * N_HEADS, d)), _resident((d, d), (mixer_layer,))]
        scratch.append(pltpu.VMEM((d // LANES, tm, LANES), F32))
        buffers += [((tm, d), BF16, 2 * N_BRANCHES), ((tm, LANES), F32, 2 * N_BRANCHES),
                    ((2 * N_HEADS, d), BF16, 1), ((d, d), BF16, 1), ((tm, d), F32, 1)]
    which = (layer, half)
    in_specs += [_resident((d, D_FF), which), _resident((d, D_FF), which), _resident((D_FF, d), which)]
    return pl.pallas_call(
        functools.partial(_ffn_kernel, mode=mode, final=final),
        grid=(t // tm,),
        in_specs=in_specs,
        out_specs=pl.BlockSpec((tm, d), row),
        out_shape=jax.ShapeDtypeStruct((t, d), F32),
        scratch_shapes=scratch,
        compiler_params=_params(("arbitrary",), *buffers),
        name="ffn_" + mode,
    )(x, vec, *mixer, *ffn_w)


def _in_head(dim, a):
    return (dim // (HEAD_DIM // 2)) % 2 == a


def _rope(y, cos, sin):
    pieces = []
    for c in range(y.shape[1] // LANES):
        yc = y[:, c * LANES:(c + 1) * LANES]
        pieces.append(yc * cos + pltpu.roll(yc, LANES // 2, 1) * sin)
    return jnp.concatenate(pieces, axis=1)


def _proj_kernel(x_ref, vec_ref, cos_ref, sin_ref, w_ref, *rest, sections, fused_out):
    o_refs, h_ref = rest[:-1], rest[-1]
    tm = x_ref.shape[0]
    vec = vec_ref[0]
    h = _norm_mod(x_ref[...], vec[0:1], vec[1:2], vec[2:3])
    dils = sorted({dil for _, _, dil in sections})
    if dils != [1]:
        _store_lane_chunks(h_ref, h)
    h_by_dil = {dil: (h if dil == 1 else _load_strided_order(h_ref, dil)).astype(BF16) for dil in dils}
    for s, (rope, scale, dil) in enumerate(sections):
        y = _dot(h_by_dil[dil], w_ref[:, s * D_MODEL:(s + 1) * D_MODEL])
        if rope:
            t_i = dils.index(dil)
            y = _rope(y, cos_ref[t_i], sin_ref[t_i])
        if scale != 1.0:
            y = y * scale
        y = y.astype(BF16)
        if fused_out:
            o_refs[0][:, s * D_MODEL:(s + 1) * D_MODEL] = y
        elif dil == 1:
            o_refs[s][...] = y
        else:
            rows = tm // dil
            for r in range(dil):
                o_refs[s][0, r] = y[r * rows:(r + 1) * rows, :]


def _proj_call(x, vec, tables, w, layer, *, sections, fused_out, bsz):
    t, d = x.shape
    n = w.shape[2]
    tm = TOKEN_TILE
    seq = t // bsz
    tiles_per_seq = seq // tm
    row = lambda i: (i, 0)
    n_tab = tables[0].shape[0]
    tab_spec = pl.BlockSpec((n_tab, tm, LANES), lambda i: (0, i % tiles_per_seq, 0))
    if fused_out:
        out_specs = [pl.BlockSpec((tm, n), row)]
        out_shape = [jax.ShapeDtypeStruct((t, n), BF16)]
    else:
        out_specs, out_shape = [], []
        for _, _, dil in sections:
            if dil == 1:
                out_specs.append(pl.BlockSpec((tm, d), row))
                out_shape.append(jax.ShapeDtypeStruct((t, d), BF16))
            else:
                out_specs.append(_strided_block_spec(dil, tm, tiles_per_seq))
                out_shape.append(jax.ShapeDtypeStruct((bsz, dil, seq // dil, d), BF16))
    return pl.pallas_call(
        functools.partial(_proj_kernel, sections=sections, fused_out=fused_out),
        grid=(t // tm,),
        in_specs=[pl.BlockSpec((tm, d), row),
                  pl.BlockSpec((1, 8, d), lambda i: (i // tiles_per_seq, 0, 0)),
                  tab_spec, tab_spec,
                  _resident((d, n), (layer,))],
        out_specs=out_specs,
        out_shape=out_shape,
        scratch_shapes=[pltpu.VMEM((d // LANES, tm, LANES), F32)],
        compiler_params=_params(("arbitrary",), ((tm, d), F32, 3), ((8, d), F32, 2),
                                ((n_tab, tm, LANES), F32, 4), ((d, n), BF16, 1), ((tm, n), BF16, 2)),
        name="proj%d" % n,
    )(x, vec, *tables, w)


def _moba_kernel(q_ref, k_ref, v_ref, o_ref, vt_ref):
    blk = MOBA_BLOCK
    n_blocks = q_ref.shape[0] // blk
    n_pairs = q_ref.shape[1] // LANES
    rows = lambda n: slice(n * blk, (n + 1) * blk)
    cols = lambda hp: slice(hp * LANES, (hp + 1) * LANES)

    k_mean = []
    for hp in range(n_pairs):
        block_means = []
        for a in range(2):
            vt_ref[2 * hp + a, HEAD_DIM:, :] = jnp.ones((ONES_ROWS, vt_ref.shape[2]), BF16)
        for n in range(n_blocks):
            v_t = v_ref[rows(n), cols(hp)].astype(F32).T.astype(BF16)
            for a in range(2):
                vt_ref[2 * hp + a, :HEAD_DIM, rows(n)] = v_t[a * HEAD_DIM:(a + 1) * HEAD_DIM, :]
            block_means.append(jnp.mean(k_ref[rows(n), cols(hp)].astype(F32), axis=0, keepdims=True))
        k_mean.append(_split_bf16(jnp.concatenate(block_means, axis=0)))

    dim = lax.broadcasted_iota(jnp.int32, (LANES, blk), 0)
    in_head = [_in_head(dim, a) for a in range(2)]
    cand = lax.broadcasted_iota(jnp.int32, (n_blocks, blk), 0)
    piece = MOBA_PIECE
    per_block = blk // piece
    key_i = lax.broadcasted_iota(jnp.int32, (piece, blk), 0)
    qry_i = lax.broadcasted_iota(jnp.int32, (piece, blk), 1)
    causal = [jnp.where(key_i + h * piece <= qry_i, 0.0, NEG_INF) for h in range(per_block)]

    def load_q(unit, _):
        qb, hp = unit
        q_t = q_ref[rows(qb), cols(hp)].astype(F32).T
        return [jnp.where(in_head[a], q_t, 0.0).astype(BF16) for a in range(2)]

    def scores(unit, q_heads):
        qb, hp = unit
        km_hi, km_lo = k_mean[hp]
        out = []
        for q_a in q_heads:
            s_all = _dot(k_ref[0:(qb + 1) * blk, cols(hp)], q_a)
            s = [s_all[i * piece:(i + 1) * piece, :] for i in range(per_block * (qb + 1))]
            gate = _dot(km_hi, q_a) + _dot(km_lo, q_a) if qb > MOBA_TOPK else None
            out.append((s, gate))
        return out

    def softmax_pv(unit, stage):
        qb, hp = unit
        return [head_softmax_pv(qb, 2 * hp + a, *stage[a]) for a in range(2)]

    def store(unit, heads):
        qb, hp = unit
        o_ref[rows(qb), cols(hp)] = jnp.concatenate(heads, axis=0).T.astype(BF16)

    def head_softmax_pv(qb, a, s, gate):
        s = list(s)
        if gate is not None:
            for n in range(qb):
                g_n = gate[n:n + 1, :]
                beats = ((gate > g_n) | ((gate == g_n) & (cand < n))) & (cand < qb)
                rank = jnp.sum(jnp.where(beats, 1.0, 0.0), axis=0, keepdims=True)
                unselected = jnp.where(rank < MOBA_TOPK, 0.0, NEG_INF)
                for h in range(per_block):
                    s[per_block * n + h] = s[per_block * n + h] + unselected
        for h in range(per_block):
            s[per_block * qb + h] = s[per_block * qb + h] + causal[h]
        m = _reduce_rows(s, jnp.maximum, jnp.max)
        p = jnp.concatenate([jnp.exp2(x - m).astype(BF16) for x in s], axis=0)
        acc = _dot(vt_ref[a, :, 0:(qb + 1) * blk], p)
        return acc[:HEAD_DIM] * (1.0 / acc[HEAD_DIM:HEAD_DIM + 1])

    units = [(qb, hp) for qb in range(n_blocks) for hp in range(n_pairs)]
    _skewed(units, (load_q, scores, softmax_pv, store), (0, 1, 2, 3))


def _moba_call(qkv, bsz):
    t, n = qkv.shape
    seq = t // bsz
    d = n // 3
    cw = MOBA_HEAD_PAIRS * LANES
    col_blocks = d // cw
    return pl.pallas_call(
        _moba_kernel,
        grid=(bsz, col_blocks),
        in_specs=[pl.BlockSpec((seq, cw), lambda b, c: (b, c)),
                  pl.BlockSpec((seq, cw), lambda b, c: (b, col_blocks + c)),
                  pl.BlockSpec((seq, cw), lambda b, c: (b, 2 * col_blocks + c))],
        out_specs=pl.BlockSpec((seq, cw), lambda b, c: (b, c)),
        out_shape=jax.ShapeDtypeStruct((t, d), BF16),
        scratch_shapes=[pltpu.VMEM((2 * MOBA_HEAD_PAIRS, HEAD_DIM + ONES_ROWS, seq), BF16)],
        compiler_params=_params(("arbitrary", "arbitrary"), ((seq, cw), BF16, 8),
                                ((2 * MOBA_HEAD_PAIRS, HEAD_DIM + ONES_ROWS, seq), BF16, 1)),
        name="moba",
    )(qkv, qkv, qkv)


def _band_kernel(q_ref, k_ref, v_ref, o_ref, lse_ref, *, blocks_per_seq):
    w = BAND
    n_blocks = q_ref.shape[0] // w
    n_pairs = q_ref.shape[1] // LANES
    key_i = lax.broadcasted_iota(jnp.int32, (w, 2 * w), 0)
    qry_i = lax.broadcasted_iota(jnp.int32, (w, 2 * w), 1) % w
    bias_of = {0: jnp.where(key_i <= qry_i, 0.0, NEG_INF), 1: jnp.where(key_i >= qry_i, 0.0, NEG_INF)}
    dim = lax.broadcasted_iota(jnp.int32, (LANES, w), 0)
    rows = lambda j: slice(j * w, (j + 1) * w)
    cols = lambda hp: slice(hp * LANES, (hp + 1) * LANES)

    def back_blocks(j):
        return (0,) if j % blocks_per_seq == 0 else (0, 1)

    v_t = {}

    def transposes(unit, _):
        j, hp = unit
        v_t[unit] = v_ref[rows(j), cols(hp)].astype(F32).T.astype(BF16)
        q_t = q_ref[rows(j), cols(hp)].astype(F32).T
        return jnp.concatenate([jnp.where(_in_head(dim, a), q_t, 0.0) for a in range(2)],
                               axis=1).astype(BF16)

    def scores(unit, q2):
        j, hp = unit
        return [_dot(k_ref[rows(j - back), cols(hp)], q2) for back in back_blocks(j)]

    def softmax_pv(unit, s_blocks):
        j, hp = unit
        s_blocks = [s + bias_of[back] for s, back in zip(s_blocks, back_blocks(j))]
        m = _reduce_rows(s_blocks, jnp.maximum, jnp.max)
        p_blocks = [jnp.exp2(s - m) for s in s_blocks]
        l = _reduce_rows(p_blocks, jnp.add, jnp.sum)
        o_t = functools.reduce(jnp.add, [_dot(v_t[j - back, hp], p.astype(BF16))
                                         for p, back in zip(p_blocks, back_blocks(j))]) * (1.0 / l)
        if j % blocks_per_seq != 0:
            del v_t[j - 1, hp]
        if (j + 1) % blocks_per_seq == 0:
            del v_t[unit]
        lse = (m + jnp.log2(l)) * LN2
        lse_ref[0, 2 * hp:2 * hp + 1, rows(j)] = lse[:, :w]
        lse_ref[0, 2 * hp + 1:2 * hp + 2, rows(j)] = lse[:, w:]
        return jnp.concatenate([o_t[:HEAD_DIM, :w], o_t[HEAD_DIM:, w:]], axis=0)

    def store(unit, o_sel):
        j, hp = unit
        o_ref[rows(j), cols(hp)] = o_sel.T.astype(BF16)

    units = [(j, hp) for j in range(n_blocks) for hp in range(n_pairs)]
    _skewed(units, (transposes, scores, softmax_pv, store), (0, 3, 6, 9))


def _band_call(q, k, v, dil, bsz):
    t, d = q.shape
    seq = t // bsz
    cw = BAND_HEAD_PAIRS * LANES
    blocks_per_seq = (seq // dil) // BAND
    blk = pl.BlockSpec((seq, cw), lambda b, c: (b, c))
    return pl.pallas_call(
        functools.partial(_band_kernel, blocks_per_seq=blocks_per_seq),
        grid=(bsz, d // cw),
        in_specs=[blk, blk, blk],
        out_specs=[blk, pl.BlockSpec((1, 2 * BAND_HEAD_PAIRS, seq), lambda b, c: (b, c, 0))],
        out_shape=[jax.ShapeDtypeStruct((t, d), BF16),
                   jax.ShapeDtypeStruct((bsz, N_HEADS, seq), F32)],
        compiler_params=_params(("arbitrary", "arbitrary"), ((seq, cw), BF16, 8),
                                ((2 * BAND_HEAD_PAIRS, seq), F32, 2)),
        name="band%d" % dil,
    )(q, k, v)


def _lse_token_order(lse, dil):
    bsz, n_h, seq = lse.shape
    return lse.reshape(bsz, n_h, dil, seq // dil).transpose(0, 3, 2, 1).reshape(bsz * seq, n_h)


def _rope_tables(seq):
    inv = 1.0 / (ROPE_THETA ** (jnp.arange(0, HEAD_DIM, 2, dtype=F32) / HEAD_DIM))
    ang = jnp.arange(seq, dtype=F32)[:, None] * inv[None, :]
    cos, sin = jnp.cos(ang), jnp.sin(ang)
    tabs = (jnp.concatenate([cos, cos, cos, cos], axis=1),
            jnp.concatenate([-sin, -sin, sin, sin], axis=1))

    def strided(tab, dil):
        tiles = seq // TOKEN_TILE
        return tab.reshape(tiles, TOKEN_TILE // dil, dil, LANES).transpose(0, 2, 1, 3).reshape(seq, LANES)

    return tuple(jnp.stack([strided(tab, dil) for dil in DILATIONS]) for tab in tabs)


def _prep_kernel(w_ref, perm_ref, o_ref, *, rope):
    for s, roped in enumerate(rope):
        for c in range(D_MODEL // LANES):
            sl = slice(s * D_MODEL + c * LANES, s * D_MODEL + (c + 1) * LANES)
            chunk = w_ref[:, sl].astype(BF16)
            o_ref[:, sl] = _dot(chunk, perm_ref[...]).astype(BF16) if roped else chunk


def _pair_split_matrix():
    half = HEAD_DIM // 2
    dst = jnp.arange(LANES)
    group, r = dst // half, dst % half
    src = (group % 2) * HEAD_DIM + (group // 2) * half + r
    return (jnp.arange(LANES)[:, None] == src[None, :]).astype(BF16)


def _prep_call(w, sections):
    n_l, d, n = w.shape
    rows = 256
    blk = pl.BlockSpec((None, rows, n), lambda l, i: (l, i, 0))
    return pl.pallas_call(
        functools.partial(_prep_kernel, rope=tuple(rope for rope, _, _ in sections)),
        grid=(n_l, d // rows),
        in_specs=[blk, pl.BlockSpec((LANES, LANES), lambda l, i: (0, 0))],
        out_specs=blk,
        out_shape=jax.ShapeDtypeStruct((n_l, d, n), BF16),
        compiler_params=_params(("arbitrary", "arbitrary"), ((rows, n), F32, 2), ((rows, n), BF16, 2),
                                ((LANES, LANES), BF16, 2)),
        name="prep%d" % n,
    )(w, _pair_split_matrix())


def _vec(bsz, *rows):
    rows = [jnp.broadcast_to(r, (bsz, D_MODEL)) for r in rows]
    rows += [jnp.zeros((bsz, D_MODEL), F32)] * (8 - len(rows))
    return jnp.stack(rows, axis=1)


def kernel(x, c, ada_w, ada_b, norm_g, ffn_w_gate, ffn_w_up, ffn_w_down, moba_w_qkv, moba_w_o,
           kv_ada_w, kv_ada_b, kv_norm_g, kv_w, dil_w_q, dil_w_o, final_g):
    bsz, seq, d = x.shape
    t = bsz * seq
    tables = _rope_tables(seq)
    natural_tables = tuple(tab[:1] for tab in tables)
    mod = _mod_call(c, ada_w, ada_b).reshape(DEPTH, bsz, 3, 3, d)
    kv_mod = _mod_call(c, kv_ada_w[None], kv_ada_b[None]).reshape(bsz, 2, d)
    head_expand = jnp.tile(jnp.repeat(jnp.eye(N_HEADS, dtype=BF16), HEAD_DIM, axis=1), (2, 1))
    q_scale = HEAD_DIM ** -0.5 * LOG2E

    ffn_w = tuple(w.astype(BF16) for w in (ffn_w_gate, ffn_w_up, ffn_w_down))
    moba_wo, dil_wo = moba_w_o.astype(BF16), dil_w_o.astype(BF16)
    qkv_sections = ((True, q_scale, 1), (True, 1.0, 1), (False, 1.0, 1))
    q_sections = tuple((True, q_scale, dil) for dil in DILATIONS)
    kv_sections = tuple((is_k, 1.0, dil) for dil in DILATIONS for is_k in (True, False))
    moba_wqkv = _prep_call(moba_w_qkv, qkv_sections)
    dil_wq = _prep_call(dil_w_q, q_sections)
    kv_wp = _prep_call(kv_w[None], kv_sections)

    xf = x.reshape(t, d)
    kvs = None
    for layer in range(DEPTH):
        m = mod[layer]
        if layer == N_A_LAYERS:
            kvs = _proj_call(xf, _vec(bsz, kv_norm_g, kv_mod[:, 0], kv_mod[:, 1]), tables, kv_wp, 0,
                             sections=kv_sections, fused_out=False, bsz=bsz)
        xf = _ffn_call(xf, _vec(bsz, norm_g[layer, 0], m[:, 0, 0], m[:, 0, 1], m[:, 0, 2]),
                       ffn_w, layer, 0)

        mix_vec = _vec(bsz, norm_g[layer, 1], m[:, 1, 0], m[:, 1, 1])
        if layer < N_A_LAYERS:
            lb = layer
            qkv, = _proj_call(xf, mix_vec, natural_tables, moba_wqkv, lb,
                              sections=qkv_sections, fused_out=True, bsz=bsz)
            mixer = (_moba_call(qkv, bsz), moba_wo)
            mode = "moba"
        else:
            lb = layer - N_A_LAYERS
            qs = _proj_call(xf, mix_vec, tables, dil_wq, lb,
                            sections=q_sections, fused_out=False, bsz=bsz)
            outs, lses = [], []
            for g, (window, dil) in enumerate(DILATED_BRANCHES):
                assert window // dil == BAND
                o_g, lse_g = _band_call(qs[g].reshape(t, d), kvs[2 * g].reshape(t, d),
                                        kvs[2 * g + 1].reshape(t, d), dil, bsz)
                outs.append(o_g if dil == 1 else o_g.reshape(bsz, dil, seq // dil, d))
                lses.append(_lse_token_order(lse_g, dil))
            mixer = (*outs, *lses, head_expand, dil_wo)
            mode = "dil"
        xf = _ffn_call(xf, _vec(bsz, norm_g[layer, 2], m[:, 2, 0], m[:, 2, 1], m[:, 2, 2],
                                m[:, 1, 2], final_g),
                       ffn_w, layer, 1, mode=mode, final=(layer == DEPTH - 1), mixer=mixer, mixer_layer=lb)
    return xf.reshape(bsz, seq, d)
```

```python
import functools
import math

import jax
import jax.numpy as jnp
from jax import lax
from jax.experimental import pallas as pl
from jax.experimental.pallas import tpu as pltpu

F32 = jnp.float32
BF16 = jnp.bfloat16

D_MODEL = 1024
HEAD_DIM = 64
N_HEADS = D_MODEL // HEAD_DIM
D_FF = 2816
ROPE_THETA = 10000.0
RMS_EPS = 1e-6
DEPTH = 4
N_A_LAYERS = 2
MOBA_BLOCK = 256
MOBA_TOPK = 3
MOBA_PIECE = 128
ONES_ROWS = 16
MOBA_HEAD_PAIRS = 2
DILATED_BRANCHES = ((128, 1), (512, 4), (2048, 16))
N_BRANCHES = len(DILATED_BRANCHES)
DILATIONS = tuple(d for _, d in DILATED_BRANCHES)
BAND = 128
LANES = 128
FFN_CHUNK = 256
TOKEN_TILE = 512
FFN_TILE = 512
BAND_HEAD_PAIRS = 4
NEG_INF = float("-inf")
LOG2E = math.log2(math.e)
LN2 = math.log(2.0)
MIB = 1024 * 1024
V7X_VMEM_BYTES = 64 * MIB


def _dot(a, b):
    return jnp.dot(a, b, preferred_element_type=F32)


def _split_bf16(a):
    hi = a.astype(BF16)
    lo = (a - hi.astype(F32)).astype(BF16)
    return hi, lo


def _resident(shape, lead=()):
    return pl.BlockSpec((None,) * len(lead) + tuple(shape), lambda *_: tuple(lead) + (0,) * len(shape),
                        pipeline_mode=pl.Buffered(1))


def _fold_rows(x, op):
    while x.shape[0] > 8 and x.shape[0] % 16 == 0:
        half = x.shape[0] // 2
        x = op(x[:half], x[half:])
    return x


def _reduce_rows(pieces, op, reduce_fn):
    folded = functools.reduce(op, [_fold_rows(x, op) for x in pieces])
    return reduce_fn(folded, axis=0, keepdims=True)


def _params(semantics, *buffers):
    declared = sum(math.prod(shape) * jnp.dtype(dtype).itemsize * count for shape, dtype, count in buffers)
    limit = min(declared + V7X_VMEM_BYTES // 4, V7X_VMEM_BYTES - 8 * MIB)
    assert declared < limit, (declared, limit)
    return pltpu.CompilerParams(dimension_semantics=semantics, vmem_limit_bytes=limit)


def _skewed(units, stages, delays):
    results = {}
    for t in range(len(units) + delays[-1]):
        for k, (stage, delay) in enumerate(zip(stages, delays)):
            i = t - delay
            if 0 <= i < len(units):
                results[k, i] = stage(units[i], results.pop((k - 1, i), None))


def _mod_kernel(c_ref, w_ref, b_ref, o_ref):
    c = c_ref[...]
    a_hi, a_lo = _split_bf16(c * jax.nn.sigmoid(c))
    w_hi, w_lo = _split_bf16(w_ref[0])
    n_b = c.shape[0]
    both = _dot(jnp.concatenate([a_hi, a_lo], axis=0), w_hi)
    o_ref[0] = both[:n_b] + both[n_b:] + _dot(a_hi, w_lo) + b_ref[0]


def _mod_call(c, w, b):
    n_l, d, n = w.shape
    bsz = c.shape[0]
    tn = 1024
    return pl.pallas_call(
        _mod_kernel,
        grid=(n_l, n // tn),
        in_specs=[pl.BlockSpec((bsz, d), lambda l, j: (0, 0)),
                  pl.BlockSpec((1, d, tn), lambda l, j: (l, 0, j)),
                  pl.BlockSpec((1, 1, tn), lambda l, j: (l, 0, j))],
        out_specs=pl.BlockSpec((1, bsz, tn), lambda l, j: (l, 0, j)),
        out_shape=jax.ShapeDtypeStruct((n_l, bsz, n), F32),
        compiler_params=_params(("arbitrary", "arbitrary"), ((bsz, d), F32, 2), ((d, tn), F32, 2),
                                ((1, tn), F32, 2), ((bsz, tn), F32, 2)),
        name="mod",
    )(c, w, b.reshape(n_l, 1, n))


def _norm_mod(x, g, shift, scale):
    ms = jnp.mean(x * x, axis=-1, keepdims=True)
    return (x * lax.rsqrt(ms + RMS_EPS) * g) * (1.0 + scale) + shift


def _store_lane_chunks(ref3, x):
    for c in range(ref3.shape[0]):
        ref3[c] = x[:, c * LANES:(c + 1) * LANES]


def _load_lane_chunks(ref3):
    return jnp.concatenate([ref3[c] for c in range(ref3.shape[0])], axis=1)


def _load_strided_order(ref3, dil):
    rows = ref3.shape[1] // dil
    cols = [jnp.concatenate([ref3[c, pl.ds(r, rows, stride=dil), :] for r in range(dil)], axis=0)
            for c in range(ref3.shape[0])]
    return jnp.concatenate(cols, axis=1)


def _store_token_order(ref3, block_ref, dil):
    rows = block_ref.shape[2]
    for r in range(dil):
        piece = block_ref[0, r].astype(F32)
        for c in range(ref3.shape[0]):
            ref3[c, pl.ds(r, rows, stride=dil), :] = piece[:, c * LANES:(c + 1) * LANES]


def _ffn_kernel(*refs, mode, final):
    it = iter(refs)
    x_ref, vec_ref = next(it), next(it)
    x = x_ref[...]
    vec = vec_ref[0]
    if mode == "moba":
        attn_ref, wo_ref = next(it), next(it)
        x = x + vec[4:5] * _dot(attn_ref[...], wo_ref[...])
    elif mode == "dil":
        o_refs = [next(it) for _ in range(N_BRANCHES)]
        l_refs = [next(it) for _ in range(N_BRANCHES)]
        e_ref, wo_ref = next(it), next(it)
    wg_ref, wu_ref, wd_ref = next(it), next(it), next(it)
    o_ref, a_ref = next(it), next(it)
    if mode == "dil":
        u_ref = next(it)
        lse = [r[...] for r in l_refs]
        mx = functools.reduce(jnp.maximum, lse)
        ex = [jnp.exp(l - mx) for l in lse]
        inv = 1.0 / functools.reduce(lambda a, b: a + b, ex)
        attn = None
        for e, br_ref, dil in zip(ex, o_refs, DILATIONS):
            if dil == 1:
                o_g = br_ref[...].astype(F32)
            else:
                _store_token_order(u_ref, br_ref, dil)
                o_g = _load_lane_chunks(u_ref)
            w_full = _dot(jnp.concatenate(_split_bf16(e * inv), axis=1), e_ref[...])
            term = w_full * o_g
            attn = term if attn is None else attn + term
        x = x + vec[4:5] * _dot(attn.astype(BF16), wo_ref[...])

    h = _norm_mod(x, vec[0:1], vec[1:2], vec[2:3]).astype(BF16)
    for c in range(D_FF // FFN_CHUNK):
        sl = slice(c * FFN_CHUNK, (c + 1) * FFN_CHUNK)
        gate = _dot(h, wg_ref[:, sl].astype(BF16))
        up = _dot(h, wu_ref[:, sl].astype(BF16))
        a_ref[:, sl] = (gate * jax.nn.sigmoid(gate) * up).astype(BF16)
    y = x + (0.5 * vec[3:4]) * _dot(a_ref[...], wd_ref[...])
    if final:
        ms = jnp.mean(y * y, axis=-1, keepdims=True)
        y = y * lax.rsqrt(ms + RMS_EPS) * vec[5:6]
    o_ref[...] = y


def _strided_block_spec(dil, tm, tiles_per_seq):
    return pl.BlockSpec((1, dil, tm // dil, D_MODEL),
                        lambda i: (i // tiles_per_seq, 0, i % tiles_per_seq, 0))


def _ffn_call(x, vec, ffn_w, layer, half, *, mode="none", final=False, mixer=(), mixer_layer=0):
    t, d = x.shape
    bsz = vec.shape[0]
    tm = TOKEN_TILE if mode == "dil" else FFN_TILE
    tiles_per_seq = (t // bsz) // tm
    row = lambda i: (i, 0)
    in_specs = [pl.BlockSpec((tm, d), row),
                pl.BlockSpec((1, 8, d), lambda i: (i // tiles_per_seq, 0, 0))]
    scratch = [pltpu.VMEM((tm, D_FF), BF16)]
    buffers = [((tm, d), F32, 4), ((8, d), F32, 2), ((tm, D_FF), BF16, 1),
               ((d, D_FF), F32, 2), ((D_FF, d), BF16, 1)]
    if mode == "moba":
        in_specs += [pl.BlockSpec((tm, d), row), _resident((d, d), (mixer_layer,))]
        buffers += [((tm, d), BF16, 2), ((d, d), BF16, 1)]
    elif mode == "dil":
        for dil in DILATIONS:
            in_specs.append(pl.BlockSpec((tm, d), row) if dil == 1
                            else _strided_block_spec(dil, tm, tiles_per_seq))
        in_specs += [pl.BlockSpec((tm, N_HEADS), row)] * N_BRANCHES
        in_specs += [_resident((2 * N_HEADS, d)), _resident((d, d), (mixer_layer,))]
        scratch.append(pltpu.VMEM((d // LANES, tm, LANES), F32))
        buffers += [((tm, d), BF16, 2 * N_BRANCHES), ((tm, LANES), F32, 2 * N_BRANCHES),
                    ((2 * N_HEADS, d), BF16, 1), ((d, d), BF16, 1), ((tm, d), F32, 1)]
    which = (layer, half)
    in_specs += [_resident((d, D_FF), which), _resident((d, D_FF), which), _resident((D_FF, d), which)]
    return pl.pallas_call(
        functools.partial(_ffn_kernel, mode=mode, final=final),
        grid=(t // tm,),
        in_specs=in_specs,
        out_specs=pl.BlockSpec((tm, d), row),
        out_shape=jax.ShapeDtypeStruct((t, d), F32),
        scratch_shapes=scratch,
        compiler_params=_params(("arbitrary",), *buffers),
        name="ffn_" + mode,
    )(x, vec, *mixer, *ffn_w)


def _in_head(dim, a):
    return (dim // (HEAD_DIM // 2)) % 2 == a


def _rope(y, cos, sin):
    pieces = []
    for c in range(y.shape[1] // LANES):
        yc = y[:, c * LANES:(c + 1) * LANES]
        pieces.append(yc * cos + pltpu.roll(yc, LANES // 2, 1) * sin)
    return jnp.concatenate(pieces, axis=1)


def _proj_kernel(x_ref, vec_ref, cos_ref, sin_ref, w_ref, *rest, sections, fused_out):
    o_refs, h_ref = rest[:-1], rest[-1]
    tm = x_ref.shape[0]
    vec = vec_ref[0]
    h = _norm_mod(x_ref[...], vec[0:1], vec[1:2], vec[2:3])
    dils = sorted({dil for _, _, dil in sections})
    if dils != [1]:
        _store_lane_chunks(h_ref, h)
    h_by_dil = {dil: (h if dil == 1 else _load_strided_order(h_ref, dil)).astype(BF16) for dil in dils}
    for s, (rope, scale, dil) in enumerate(sections):
        y = _dot(h_by_dil[dil], w_ref[:, s * D_MODEL:(s + 1) * D_MODEL])
        if rope:
            t_i = dils.index(dil)
            y = _rope(y, cos_ref[t_i], sin_ref[t_i])
        if scale != 1.0:
            y = y * scale
        y = y.astype(BF16)
        if fused_out:
            o_refs[0][:, s * D_MODEL:(s + 1) * D_MODEL] = y
        elif dil == 1:
            o_refs[s][...] = y
        else:
            rows = tm // dil
            for r in range(dil):
                o_refs[s][0, r] = y[r * rows:(r + 1) * rows, :]


def _proj_call(x, vec, tables, w, layer, *, sections, fused_out, bsz):
    t, d = x.shape
    n = w.shape[2]
    tm = TOKEN_TILE
    seq = t // bsz
    tiles_per_seq = seq // tm
    row = lambda i: (i, 0)
    n_tab = tables[0].shape[0]
    tab_spec = pl.BlockSpec((n_tab, tm, LANES), lambda i: (0, i % tiles_per_seq, 0))
    if fused_out:
        out_specs = [pl.BlockSpec((tm, n), row)]
        out_shape = [jax.ShapeDtypeStruct((t, n), BF16)]
    else:
        out_specs, out_shape = [], []
        for _, _, dil in sections:
            if dil == 1:
                out_specs.append(pl.BlockSpec((tm, d), row))
                out_shape.append(jax.ShapeDtypeStruct((t, d), BF16))
            else:
                out_specs.append(_strided_block_spec(dil, tm, tiles_per_seq))
                out_shape.append(jax.ShapeDtypeStruct((bsz, dil, seq // dil, d), BF16))
    return pl.pallas_call(
        functools.partial(_proj_kernel, sections=sections, fused_out=fused_out),
        grid=(t // tm,),
        in_specs=[pl.BlockSpec((tm, d), row),
                  pl.BlockSpec((1, 8, d), lambda i: (i // tiles_per_seq, 0, 0)),
                  tab_spec, tab_spec,
                  _resident((d, n), (layer,))],
        out_specs=out_specs,
        out_shape=out_shape,
        scratch_shapes=[pltpu.VMEM((d // LANES, tm, LANES), F32)],
        compiler_params=_params(("arbitrary",), ((tm, d), F32, 3), ((8, d), F32, 2),
                                ((n_tab, tm, LANES), F32, 4), ((d, n), BF16, 1), ((tm, n), BF16, 2)),
        name="proj%d" % n,
    )(x, vec, *tables, w)


def _moba_kernel(q_ref, k_ref, v_ref, o_ref, vt_ref):
    blk = MOBA_BLOCK
    n_blocks = q_ref.shape[0] // blk
    n_pairs = q_ref.shape[1] // LANES
    rows = lambda n: slice(n * blk, (n + 1) * blk)
    cols = lambda hp: slice(hp * LANES, (hp + 1) * LANES)

    k_mean = []
    for hp in range(n_pairs):
        block_means = []
        for a in range(2):
            vt_ref[2 * hp + a, HEAD_DIM:, :] = jnp.ones((ONES_ROWS, vt_ref.shape[2]), BF16)
        for n in range(n_blocks):
            v_t = v_ref[rows(n), cols(hp)].astype(F32).T.astype(BF16)
            for a in range(2):
                vt_ref[2 * hp + a, :HEAD_DIM, rows(n)] = v_t[a * HEAD_DIM:(a + 1) * HEAD_DIM, :]
            block_means.append(jnp.mean(k_ref[rows(n), cols(hp)].astype(F32), axis=0, keepdims=True))
        k_mean.append(_split_bf16(jnp.concatenate(block_means, axis=0)))

    dim = lax.broadcasted_iota(jnp.int32, (LANES, blk), 0)
    in_head = [_in_head(dim, a) for a in range(2)]
    cand = lax.broadcasted_iota(jnp.int32, (n_blocks, blk), 0)
    piece = MOBA_PIECE
    per_block = blk // piece
    key_i = lax.broadcasted_iota(jnp.int32, (piece, blk), 0)
    qry_i = lax.broadcasted_iota(jnp.int32, (piece, blk), 1)
    causal = [jnp.where(key_i + h * piece <= qry_i, 0.0, NEG_INF) for h in range(per_block)]

    def load_q(unit, _):
        qb, hp = unit
        q_t = q_ref[rows(qb), cols(hp)].astype(F32).T
        return [jnp.where(in_head[a], q_t, 0.0).astype(BF16) for a in range(2)]

    def scores(unit, q_heads):
        qb, hp = unit
        km_hi, km_lo = k_mean[hp]
        out = []
        for q_a in q_heads:
            s_all = _dot(k_ref[0:(qb + 1) * blk, cols(hp)], q_a)
            s = [s_all[i * piece:(i + 1) * piece, :] for i in range(per_block * (qb + 1))]
            gate = _dot(km_hi, q_a) + _dot(km_lo, q_a) if qb > MOBA_TOPK else None
            out.append((s, gate))
        return out

    def softmax_pv(unit, stage):
        qb, hp = unit
        return [head_softmax_pv(qb, 2 * hp + a, *stage[a]) for a in range(2)]

    def store(unit, heads):
        qb, hp = unit
        o_ref[rows(qb), cols(hp)] = jnp.concatenate(heads, axis=0).T.astype(BF16)

    def head_softmax_pv(qb, a, s, gate):
        s = list(s)
        if gate is not None:
            for n in range(qb):
                g_n = gate[n:n + 1, :]
                beats = ((gate > g_n) | ((gate == g_n) & (cand < n))) & (cand < qb)
                rank = jnp.sum(jnp.where(beats, 1.0, 0.0), axis=0, keepdims=True)
                unselected = jnp.where(rank < MOBA_TOPK, 0.0, NEG_INF)
                for h in range(per_block):
                    s[per_block * n + h] = s[per_block * n + h] + unselected
        for h in range(per_block):
            s[per_block * qb + h] = s[per_block * qb + h] + causal[h]
        m = _reduce_rows(s, jnp.maximum, jnp.max)
        p = jnp.concatenate([jnp.exp2(x - m).astype(BF16) for x in s], axis=0)
        acc = _dot(vt_ref[a, :, 0:(qb + 1) * blk], p)
        return acc[:HEAD_DIM] * (1.0 / acc[HEAD_DIM:HEAD_DIM + 1])

    units = [(qb, hp) for qb in range(n_blocks) for hp in range(n_pairs)]
    _skewed(units, (load_q, scores, softmax_pv, store), (0, 1, 2, 3))


def _moba_call(qkv, bsz):
    t, n = qkv.shape
    seq = t // bsz
    d = n // 3
    cw = MOBA_HEAD_PAIRS * LANES
    col_blocks = d // cw
    return pl.pallas_call(
        _moba_kernel,
        grid=(bsz, col_blocks),
        in_specs=[pl.BlockSpec((seq, cw), lambda b, c: (b, c)),
                  pl.BlockSpec((seq, cw), lambda b, c: (b, col_blocks + c)),
                  pl.BlockSpec((seq, cw), lambda b, c: (b, 2 * col_blocks + c))],
        out_specs=pl.BlockSpec((seq, cw), lambda b, c: (b, c)),
        out_shape=jax.ShapeDtypeStruct((t, d), BF16),
        scratch_shapes=[pltpu.VMEM((2 * MOBA_HEAD_PAIRS, HEAD_DIM + ONES_ROWS, seq), BF16)],
        compiler_params=_params(("arbitrary", "arbitrary"), ((seq, cw), BF16, 8),
                                ((2 * MOBA_HEAD_PAIRS, HEAD_DIM + ONES_ROWS, seq), BF16, 1)),
        name="moba",
    )(qkv, qkv, qkv)


def _band_kernel(q_ref, k_ref, v_ref, o_ref, lse_ref, *, blocks_per_seq):
    w = BAND
    n_blocks = q_ref.shape[0] // w
    n_pairs = q_ref.shape[1] // LANES
    key_i = lax.broadcasted_iota(jnp.int32, (w, 2 * w), 0)
    qry_i = lax.broadcasted_iota(jnp.int32, (w, 2 * w), 1) % w
    bias_of = {0: jnp.where(key_i <= qry_i, 0.0, NEG_INF), 1: jnp.where(key_i >= qry_i, 0.0, NEG_INF)}
    dim = lax.broadcasted_iota(jnp.int32, (LANES, w), 0)
    rows = lambda j: slice(j * w, (j + 1) * w)
    cols = lambda hp: slice(hp * LANES, (hp + 1) * LANES)

    def back_blocks(j):
        return (0,) if j % blocks_per_seq == 0 else (0, 1)

    v_t = {}

    def transposes(unit, _):
        j, hp = unit
        v_t[unit] = v_ref[rows(j), cols(hp)].astype(F32).T.astype(BF16)
        q_t = q_ref[rows(j), cols(hp)].astype(F32).T
        return jnp.concatenate([jnp.where(_in_head(dim, a), q_t, 0.0) for a in range(2)],
                               axis=1).astype(BF16)

    def scores(unit, q2):
        j, hp = unit
        return [_dot(k_ref[rows(j - back), cols(hp)], q2) for back in back_blocks(j)]

    def softmax_pv(unit, s_blocks):
        j, hp = unit
        s_blocks = [s + bias_of[back] for s, back in zip(s_blocks, back_blocks(j))]
        m = _reduce_rows(s_blocks, jnp.maximum, jnp.max)
        p_blocks = [jnp.exp2(s - m) for s in s_blocks]
        l = _reduce_rows(p_blocks, jnp.add, jnp.sum)
        o_t = functools.reduce(jnp.add, [_dot(v_t[j - back, hp], p.astype(BF16))
                                         for p, back in zip(p_blocks, back_blocks(j))]) * (1.0 / l)
        if j % blocks_per_seq != 0:
            del v_t[j - 1, hp]
        if (j + 1) % blocks_per_seq == 0:
            del v_t[unit]
        lse = (m + jnp.log2(l)) * LN2
        lse_ref[0, 2 * hp:2 * hp + 1, rows(j)] = lse[:, :w]
        lse_ref[0, 2 * hp + 1:2 * hp + 2, rows(j)] = lse[:, w:]
        return jnp.concatenate([o_t[:HEAD_DIM, :w], o_t[HEAD_DIM:, w:]], axis=0)

    def store(unit, o_sel):
        j, hp = unit
        o_ref[rows(j), cols(hp)] = o_sel.T.astype(BF16)

    units = [(j, hp) for j in range(n_blocks) for hp in range(n_pairs)]
    _skewed(units, (transposes, scores, softmax_pv, store), (0, 3, 6, 9))


def _band_call(q, k, v, dil, bsz):
    t, d = q.shape
    seq = t // bsz
    cw = BAND_HEAD_PAIRS * LANES
    blocks_per_seq = (seq // dil) // BAND
    blk = pl.BlockSpec((seq, cw), lambda b, c: (b, c))
    return pl.pallas_call(
        functools.partial(_band_kernel, blocks_per_seq=blocks_per_seq),
        grid=(bsz, d // cw),
        in_specs=[blk, blk, blk],
        out_specs=[blk, pl.BlockSpec((1, 2 * BAND_HEAD_PAIRS, seq), lambda b, c: (b, c, 0))],
        out_shape=[jax.ShapeDtypeStruct((t, d), BF16),
                   jax.ShapeDtypeStruct((bsz, N_HEADS, seq), F32)],
        compiler_params=_params(("arbitrary", "arbitrary"), ((seq, cw), BF16, 8),
                                ((2 * BAND_HEAD_PAIRS, seq), F32, 2)),
        name="band%d" % dil,
    )(q, k, v)


def _lse_token_order(lse, dil):
    bsz, n_h, seq = lse.shape
    return lse.reshape(bsz, n_h, dil, seq // dil).transpose(0, 3, 2, 1).reshape(bsz * seq, n_h)


def _rope_tables(seq):
    inv = 1.0 / (ROPE_THETA ** (jnp.arange(0, HEAD_DIM, 2, dtype=F32) / HEAD_DIM))
    ang = jnp.arange(seq, dtype=F32)[:, None] * inv[None, :]
    cos, sin = jnp.cos(ang), jnp.sin(ang)
    tabs = (jnp.concatenate([cos, cos, cos, cos], axis=1),
            jnp.concatenate([-sin, -sin, sin, sin], axis=1))

    def strided(tab, dil):
        tiles = seq // TOKEN_TILE
        return tab.reshape(tiles, TOKEN_TILE // dil, dil, LANES).transpose(0, 2, 1, 3).reshape(seq, LANES)

    return tuple(jnp.stack([strided(tab, dil) for dil in DILATIONS]) for tab in tabs)


def _prep_kernel(w_ref, perm_ref, o_ref, *, rope):
    for s, roped in enumerate(rope):
        for c in range(D_MODEL // LANES):
            sl = slice(s * D_MODEL + c * LANES, s * D_MODEL + (c + 1) * LANES)
            chunk = w_ref[:, sl].astype(BF16)
            o_ref[:, sl] = _dot(chunk, perm_ref[...]).astype(BF16) if roped else chunk


def _pair_split_matrix():
    half = HEAD_DIM // 2
    dst = jnp.arange(LANES)
    group, r = dst // half, dst % half
    src = (group % 2) * HEAD_DIM + (group // 2) * half + r
    return (jnp.arange(LANES)[:, None] == src[None, :]).astype(BF16)


def _prep_call(w, sections):
    n_l, d, n = w.shape
    rows = 256
    blk = pl.BlockSpec((None, rows, n), lambda l, i: (l, i, 0))
    return pl.pallas_call(
        functools.partial(_prep_kernel, rope=tuple(rope for rope, _, _ in sections)),
        grid=(n_l, d // rows),
        in_specs=[blk, pl.BlockSpec((LANES, LANES), lambda l, i: (0, 0))],
        out_specs=blk,
        out_shape=jax.ShapeDtypeStruct((n_l, d, n), BF16),
        compiler_params=_params(("arbitrary", "arbitrary"), ((rows, n), F32, 2), ((rows, n), BF16, 2),
                                ((LANES, LANES), BF16, 2)),
        name="prep%d" % n,
    )(w, _pair_split_matrix())


def _vec(bsz, *rows):
    rows = [jnp.broadcast_to(r, (bsz, D_MODEL)) for r in rows]
    rows += [jnp.zeros((bsz, D_MODEL), F32)] * (8 - len(rows))
    return jnp.stack(rows, axis=1)


def kernel(x, c, ada_w, ada_b, norm_g, ffn_w_gate, ffn_w_up, ffn_w_down, moba_w_qkv, moba_w_o,
           kv_ada_w, kv_ada_b, kv_norm_g, kv_w, dil_w_q, dil_w_o, final_g):
    bsz, seq, d = x.shape
    t = bsz * seq
    tables = _rope_tables(seq)
    natural_tables = tuple(tab[:1] for tab in tables)
    mod = _mod_call(c, ada_w, ada_b).reshape(DEPTH, bsz, 3, 3, d)
    kv_mod = _mod_call(c, kv_ada_w[None], kv_ada_b[None]).reshape(bsz, 2, d)
    head_expand = jnp.tile(jnp.repeat(jnp.eye(N_HEADS, dtype=BF16), HEAD_DIM, axis=1), (2, 1))
    q_scale = HEAD_DIM ** -0.5 * LOG2E

    ffn_w = (ffn_w_gate, ffn_w_up, ffn_w_down.astype(BF16))
    moba_wo, dil_wo = moba_w_o.astype(BF16), dil_w_o.astype(BF16)
    qkv_sections = ((True, q_scale, 1), (True, 1.0, 1), (False, 1.0, 1))
    q_sections = tuple((True, q_scale, dil) for dil in DILATIONS)
    kv_sections = tuple((is_k, 1.0, dil) for dil in DILATIONS for is_k in (True, False))
    moba_wqkv = _prep_call(moba_w_qkv, qkv_sections)
    dil_wq = _prep_call(dil_w_q, q_sections)
    kv_wp = _prep_call(kv_w[None], kv_sections)

    xf = x.reshape(t, d)
    kvs = None
    for layer in range(DEPTH):
        m = mod[layer]
        if layer == N_A_LAYERS:
            kvs = _proj_call(xf, _vec(bsz, kv_norm_g, kv_mod[:, 0], kv_mod[:, 1]), tables, kv_wp, 0,
                             sections=kv_sections, fused_out=False, bsz=bsz)
        xf = _ffn_call(xf, _vec(bsz, norm_g[layer, 0], m[:, 0, 0], m[:, 0, 1], m[:, 0, 2]),
                       ffn_w, layer, 0)

        mix_vec = _vec(bsz, norm_g[layer, 1], m[:, 1, 0], m[:, 1, 1])
        if layer < N_A_LAYERS:
            lb = layer
            qkv, = _proj_call(xf, mix_vec, natural_tables, moba_wqkv, lb,
                              sections=qkv_sections, fused_out=True, bsz=bsz)
            mixer = (_moba_call(qkv, bsz), moba_wo)
            mode = "moba"
        else:
            lb = layer - N_A_LAYERS
            qs = _proj_call(xf, mix_vec, tables, dil_wq, lb,
                            sections=q_sections, fused_out=False, bsz=bsz)
            outs, lses = [], []
            for g, (window, dil) in enumerate(DILATED_BRANCHES):
                assert window // dil == BAND
                o_g, lse_g = _band_call(qs[g].reshape(t, d), kvs[2 * g].reshape(t, d),
                                        kvs[2 * g + 1].reshape(t, d), dil, bsz)
                outs.append(o_g if dil == 1 else o_g.reshape(bsz, dil, seq // dil, d))
                lses.append(_lse_token_order(lse_g, dil))
            mixer = (*outs, *lses, head_expand, dil_wo)
            mode = "dil"
        xf = _ffn_call(xf, _vec(bsz, norm_g[layer, 2], m[:, 2, 0], m[:, 2, 1], m[:, 2, 2],
                                m[:, 1, 2], final_g),
                       ffn_w, layer, 1, mode=mode, final=(layer == DEPTH - 1), mixer=mixer, mixer_layer=lb)
    return xf.reshape(bsz, seq, d)
```

```python
import functools
import math

import jax
import jax.numpy as jnp
from jax import lax
from jax.experimental import pallas as pl
from jax.experimental.pallas import tpu as pltpu

F32 = jnp.float32
BF16 = jnp.bfloat16

D_MODEL = 1024
HEAD_DIM = 64
N_HEADS = D_MODEL // HEAD_DIM
D_FF = 2816
ROPE_THETA = 10000.0
RMS_EPS = 1e-6
DEPTH = 4
N_A_LAYERS = 2
MOBA_BLOCK = 256
MOBA_TOPK = 3
MOBA_PIECE = 128
ONES_ROWS = 16
MOBA_HEAD_PAIRS = 2
DILATED_BRANCHES = ((128, 1), (512, 4), (2048, 16))
N_BRANCHES = len(DILATED_BRANCHES)
DILATIONS = tuple(d for _, d in DILATED_BRANCHES)
BAND = 128
LANES = 128
SUBLANE_STRIDE = 4
FFN_CHUNK = 256
TOKEN_TILE = 512
FFN_TILE = 512
BAND_HEAD_PAIRS = 4
NEG_INF = float("-inf")
LOG2E = math.log2(math.e)
LN2 = math.log(2.0)
MIB = 1024 * 1024
V7X_VMEM_BYTES = 64 * MIB


def _dot(a, b):
    return jnp.dot(a, b, preferred_element_type=F32)


def _split_bf16(a):
    hi = a.astype(BF16)
    lo = (a - hi.astype(F32)).astype(BF16)
    return hi, lo


def _resident(shape, lead=()):
    return pl.BlockSpec((None,) * len(lead) + tuple(shape), lambda *_: tuple(lead) + (0,) * len(shape),
                        pipeline_mode=pl.Buffered(1))


def _fold_rows(x, op):
    while x.shape[0] > 8 and x.shape[0] % 16 == 0:
        half = x.shape[0] // 2
        x = op(x[:half], x[half:])
    return x


def _reduce_rows(pieces, op, reduce_fn):
    folded = functools.reduce(op, [_fold_rows(x, op) for x in pieces])
    return reduce_fn(folded, axis=0, keepdims=True)


def _params(semantics, *buffers):
    declared = sum(math.prod(shape) * jnp.dtype(dtype).itemsize * count for shape, dtype, count in buffers)
    limit = min(declared + V7X_VMEM_BYTES // 4, V7X_VMEM_BYTES - 8 * MIB)
    assert declared < limit, (declared, limit)
    return pltpu.CompilerParams(dimension_semantics=semantics, vmem_limit_bytes=limit)


def _skewed(units, stages, delays):
    results = {}
    for t in range(len(units) + delays[-1]):
        for k, (stage, delay) in enumerate(zip(stages, delays)):
            i = t - delay
            if 0 <= i < len(units):
                results[k, i] = stage(units[i], results.pop((k - 1, i), None))


def _mod_kernel(c_ref, w_ref, b_ref, o_ref):
    c = c_ref[...]
    a_hi, a_lo = _split_bf16(c * jax.nn.sigmoid(c))
    w_hi, w_lo = _split_bf16(w_ref[0])
    n_b = c.shape[0]
    both = _dot(jnp.concatenate([a_hi, a_lo], axis=0), w_hi)
    o_ref[0] = both[:n_b] + both[n_b:] + _dot(a_hi, w_lo) + b_ref[0]


def _mod_call(c, w, b):
    n_l, d, n = w.shape
    bsz = c.shape[0]
    tn = 1024
    return pl.pallas_call(
        _mod_kernel,
        grid=(n_l, n // tn),
        in_specs=[pl.BlockSpec((bsz, d), lambda l, j: (0, 0)),
                  pl.BlockSpec((1, d, tn), lambda l, j: (l, 0, j)),
                  pl.BlockSpec((1, 1, tn), lambda l, j: (l, 0, j))],
        out_specs=pl.BlockSpec((1, bsz, tn), lambda l, j: (l, 0, j)),
        out_shape=jax.ShapeDtypeStruct((n_l, bsz, n), F32),
        compiler_params=_params(("arbitrary", "arbitrary"), ((bsz, d), F32, 2), ((d, tn), F32, 2),
                                ((1, tn), F32, 2), ((bsz, tn), F32, 2)),
        name="mod",
    )(c, w, b.reshape(n_l, 1, n))


def _norm_mod(x, g, shift, scale):
    ms = jnp.mean(x * x, axis=-1, keepdims=True)
    return (x * lax.rsqrt(ms + RMS_EPS) * g) * (1.0 + scale) + shift


def _store_lane_chunks(ref3, x):
    for c in range(ref3.shape[0]):
        ref3[c] = x[:, c * LANES:(c + 1) * LANES]


def _load_lane_chunks(ref3):
    return jnp.concatenate([ref3[c] for c in range(ref3.shape[0])], axis=1)


def _load_strided_order(ref3, dil, tmp3=None, coarse=None):
    n_rows = ref3.shape[1]
    if dil <= SUBLANE_STRIDE:
        cols = [jnp.concatenate([ref3[c, pl.ds(r, n_rows // dil, stride=dil), :] for r in range(dil)], axis=0)
                for c in range(ref3.shape[0])]
        return jnp.concatenate(cols, axis=1)
    f = SUBLANE_STRIDE
    assert dil == f * f
    _store_lane_chunks(tmp3, _load_strided_order(ref3, f) if coarse is None else coarse)
    part = n_rows // f
    cols = [jnp.concatenate([tmp3[c, pl.ds((r % f) * part + r // f, n_rows // dil, stride=f), :]
                             for r in range(dil)], axis=0)
            for c in range(ref3.shape[0])]
    return jnp.concatenate(cols, axis=1)


def _store_token_order(ref3, block_ref, dil, tmp3=None):
    rows = block_ref.shape[2]
    f = SUBLANE_STRIDE
    two_pass = dil > f
    assert not two_pass or dil == f * f
    part = ref3.shape[1] // f
    for r in range(dil):
        piece = block_ref[0, r].astype(F32)
        for c in range(ref3.shape[0]):
            chunk = piece[:, c * LANES:(c + 1) * LANES]
            if two_pass:
                tmp3[c, pl.ds((r % f) * part + r // f, rows, stride=f), :] = chunk
            else:
                ref3[c, pl.ds(r, rows, stride=dil), :] = chunk
    if two_pass:
        for b in range(f):
            for c in range(ref3.shape[0]):
                ref3[c, pl.ds(b, part, stride=f), :] = tmp3[c, b * part:(b + 1) * part, :]


def _ffn_kernel(*refs, mode, final):
    it = iter(refs)
    x_ref, vec_ref = next(it), next(it)
    x = x_ref[...]
    vec = vec_ref[0]
    if mode == "moba":
        attn_ref, wo_ref = next(it), next(it)
        x = x + vec[4:5] * _dot(attn_ref[...], wo_ref[...])
    elif mode == "dil":
        o_refs = [next(it) for _ in range(N_BRANCHES)]
        l_refs = [next(it) for _ in range(N_BRANCHES)]
        e_ref, wo_ref = next(it), next(it)
    wg_ref, wu_ref, wd_ref = next(it), next(it), next(it)
    o_ref, a_ref = next(it), next(it)
    if mode == "dil":
        u_ref, tmp_ref = next(it), next(it)
        lse = [r[...] for r in l_refs]
        mx = functools.reduce(jnp.maximum, lse)
        ex = [jnp.exp(l - mx) for l in lse]
        inv = 1.0 / functools.reduce(lambda a, b: a + b, ex)
        attn = None
        for e, br_ref, dil in zip(ex, o_refs, DILATIONS):
            if dil == 1:
                o_g = br_ref[...].astype(F32)
            else:
                _store_token_order(u_ref, br_ref, dil, tmp_ref)
                o_g = _load_lane_chunks(u_ref)
            w_full = _dot(jnp.concatenate(_split_bf16(e * inv), axis=1), e_ref[...])
            term = w_full * o_g
            attn = term if attn is None else attn + term
        x = x + vec[4:5] * _dot(attn.astype(BF16), wo_ref[...])

    h = _norm_mod(x, vec[0:1], vec[1:2], vec[2:3]).astype(BF16)
    for c in range(D_FF // FFN_CHUNK):
        sl = slice(c * FFN_CHUNK, (c + 1) * FFN_CHUNK)
        gate = _dot(h, wg_ref[:, sl].astype(BF16))
        up = _dot(h, wu_ref[:, sl].astype(BF16))
        a_ref[:, sl] = (gate * jax.nn.sigmoid(gate) * up).astype(BF16)
    y = x + (0.5 * vec[3:4]) * _dot(a_ref[...], wd_ref[...])
    if final:
        ms = jnp.mean(y * y, axis=-1, keepdims=True)
        y = y * lax.rsqrt(ms + RMS_EPS) * vec[5:6]
    o_ref[...] = y


def _strided_block_spec(dil, tm, tiles_per_seq):
    return pl.BlockSpec((1, dil, tm // dil, D_MODEL),
                        lambda i: (i // tiles_per_seq, 0, i % tiles_per_seq, 0))


def _ffn_call(x, vec, ffn_w, layer, half, *, mode="none", final=False, mixer=(), mixer_layer=0):
    t, d = x.shape
    bsz = vec.shape[0]
    tm = TOKEN_TILE if mode == "dil" else FFN_TILE
    tiles_per_seq = (t // bsz) // tm
    row = lambda i: (i, 0)
    in_specs = [pl.BlockSpec((tm, d), row),
                pl.BlockSpec((1, 8, d), lambda i: (i // tiles_per_seq, 0, 0))]
    scratch = [pltpu.VMEM((tm, D_FF), BF16)]
    buffers = [((tm, d), F32, 4), ((8, d), F32, 2), ((tm, D_FF), BF16, 1),
               ((d, D_FF), F32, 2), ((D_FF, d), BF16, 1)]
    if mode == "moba":
        in_specs += [pl.BlockSpec((tm, d), row), _resident((d, d), (mixer_layer,))]
        buffers += [((tm, d), BF16, 2), ((d, d), BF16, 1)]
    elif mode == "dil":
        for dil in DILATIONS:
            in_specs.append(pl.BlockSpec((tm, d), row) if dil == 1
                            else _strided_block_spec(dil, tm, tiles_per_seq))
        in_specs += [pl.BlockSpec((tm, N_HEADS), row)] * N_BRANCHES
        in_specs += [_resident((2 * N_HEADS, d)), _resident((d, d), (mixer_layer,))]
        scratch += [pltpu.VMEM((d // LANES, tm, LANES), F32)] * 2
        buffers += [((tm, d), BF16, 2 * N_BRANCHES), ((tm, LANES), F32, 2 * N_BRANCHES),
                    ((2 * N_HEADS, d), BF16, 1), ((d, d), BF16, 1), ((tm, d), F32, 2)]
    which = (layer, half)
    in_specs += [_resident((d, D_FF), which), _resident((d, D_FF), which), _resident((D_FF, d), which)]
    return pl.pallas_call(
        functools.partial(_ffn_kernel, mode=mode, final=final),
        grid=(t // tm,),
        in_specs=in_specs,
        out_specs=pl.BlockSpec((tm, d), row),
        out_shape=jax.ShapeDtypeStruct((t, d), F32),
        scratch_shapes=scratch,
        compiler_params=_params(("arbitrary",), *buffers),
        name="ffn_" + mode,
    )(x, vec, *mixer, *ffn_w)


def _in_head(dim, a):
    return (dim // (HEAD_DIM // 2)) % 2 == a


def _rope(y, cos, sin):
    pieces = []
    for c in range(y.shape[1] // LANES):
        yc = y[:, c * LANES:(c + 1) * LANES]
        pieces.append(yc * cos + pltpu.roll(yc, LANES // 2, 1) * sin)
    return jnp.concatenate(pieces, axis=1)


def _proj_kernel(x_ref, vec_ref, cos_ref, sin_ref, w_ref, *rest, sections, fused_out):
    o_refs, h_ref, tmp_ref = rest[:-2], rest[-2], rest[-1]
    tm = x_ref.shape[0]
    vec = vec_ref[0]
    h = _norm_mod(x_ref[...], vec[0:1], vec[1:2], vec[2:3])
    dils = sorted({dil for _, _, dil in sections})
    if dils != [1]:
        _store_lane_chunks(h_ref, h)
    h_f32 = {1: h}
    for dil in dils[1:]:
        h_f32[dil] = _load_strided_order(h_ref, dil, tmp_ref, h_f32.get(SUBLANE_STRIDE))
    h_by_dil = {dil: h_f32[dil].astype(BF16) for dil in dils}
    for s, (rope, scale, dil) in enumerate(sections):
        y = _dot(h_by_dil[dil], w_ref[:, s * D_MODEL:(s + 1) * D_MODEL])
        if rope:
            t_i = dils.index(dil)
            y = _rope(y, cos_ref[t_i], sin_ref[t_i])
        if scale != 1.0:
            y = y * scale
        y = y.astype(BF16)
        if fused_out:
            o_refs[0][:, s * D_MODEL:(s + 1) * D_MODEL] = y
        elif dil == 1:
            o_refs[s][...] = y
        else:
            rows = tm // dil
            for r in range(dil):
                o_refs[s][0, r] = y[r * rows:(r + 1) * rows, :]


def _proj_call(x, vec, tables, w, layer, *, sections, fused_out, bsz):
    t, d = x.shape
    n = w.shape[2]
    tm = TOKEN_TILE
    seq = t // bsz
    tiles_per_seq = seq // tm
    row = lambda i: (i, 0)
    n_tab = tables[0].shape[0]
    tab_spec = pl.BlockSpec((n_tab, tm, LANES), lambda i: (0, i % tiles_per_seq, 0))
    if fused_out:
        out_specs = [pl.BlockSpec((tm, n), row)]
        out_shape = [jax.ShapeDtypeStruct((t, n), BF16)]
    else:
        out_specs, out_shape = [], []
        for _, _, dil in sections:
            if dil == 1:
                out_specs.append(pl.BlockSpec((tm, d), row))
                out_shape.append(jax.ShapeDtypeStruct((t, d), BF16))
            else:
                out_specs.append(_strided_block_spec(dil, tm, tiles_per_seq))
                out_shape.append(jax.ShapeDtypeStruct((bsz, dil, seq // dil, d), BF16))
    return pl.pallas_call(
        functools.partial(_proj_kernel, sections=sections, fused_out=fused_out),
        grid=(t // tm,),
        in_specs=[pl.BlockSpec((tm, d), row),
                  pl.BlockSpec((1, 8, d), lambda i: (i // tiles_per_seq, 0, 0)),
                  tab_spec, tab_spec,
                  _resident((d, n), (layer,))],
        out_specs=out_specs,
        out_shape=out_shape,
        scratch_shapes=[pltpu.VMEM((d // LANES, tm, LANES), F32)] * 2,
        compiler_params=_params(("arbitrary",), ((tm, d), F32, 3), ((8, d), F32, 2),
                                ((n_tab, tm, LANES), F32, 4), ((d, n), BF16, 1), ((tm, n), BF16, 2), ((tm, d), F32, 1)),
        name="proj%d" % n,
    )(x, vec, *tables, w)


def _moba_kernel(q_ref, k_ref, v_ref, o_ref, vt_ref):
    blk = MOBA_BLOCK
    n_blocks = q_ref.shape[0] // blk
    n_pairs = q_ref.shape[1] // LANES
    rows = lambda n: slice(n * blk, (n + 1) * blk)
    cols = lambda hp: slice(hp * LANES, (hp + 1) * LANES)

    k_mean = []
    for hp in range(n_pairs):
        block_means = []
        for a in range(2):
            vt_ref[2 * hp + a, HEAD_DIM:, :] = jnp.ones((ONES_ROWS, vt_ref.shape[2]), BF16)
        for n in range(n_blocks):
            v_t = v_ref[rows(n), cols(hp)].astype(F32).T.astype(BF16)
            for a in range(2):
                vt_ref[2 * hp + a, :HEAD_DIM, rows(n)] = v_t[a * HEAD_DIM:(a + 1) * HEAD_DIM, :]
            block_means.append(jnp.mean(k_ref[rows(n), cols(hp)].astype(F32), axis=0, keepdims=True))
        k_mean.append(_split_bf16(jnp.concatenate(block_means, axis=0)))

    dim = lax.broadcasted_iota(jnp.int32, (LANES, blk), 0)
    in_head = [_in_head(dim, a) for a in range(2)]
    cand = lax.broadcasted_iota(jnp.int32, (n_blocks, blk), 0)
    piece = MOBA_PIECE
    per_block = blk // piece
    key_i = lax.broadcasted_iota(jnp.int32, (piece, blk), 0)
    qry_i = lax.broadcasted_iota(jnp.int32, (piece, blk), 1)
    causal = [jnp.where(key_i + h * piece <= qry_i, 0.0, NEG_INF) for h in range(per_block)]

    def load_q(unit, _):
        qb, hp = unit
        q_t = q_ref[rows(qb), cols(hp)].astype(F32).T
        return [jnp.where(in_head[a], q_t, 0.0).astype(BF16) for a in range(2)]

    def scores(unit, q_heads):
        qb, hp = unit
        km_hi, km_lo = k_mean[hp]
        out = []
        for q_a in q_heads:
            s_all = _dot(k_ref[0:(qb + 1) * blk, cols(hp)], q_a)
            s = [s_all[i * piece:(i + 1) * piece, :] for i in range(per_block * (qb + 1))]
            gate = _dot(km_hi, q_a) + _dot(km_lo, q_a) if qb > MOBA_TOPK else None
            out.append((s, gate))
        return out

    def softmax_pv(unit, stage):
        qb, hp = unit
        return [head_softmax_pv(qb, 2 * hp + a, *stage[a]) for a in range(2)]

    def store(unit, heads):
        qb, hp = unit
        o_ref[rows(qb), cols(hp)] = jnp.concatenate(heads, axis=0).T.astype(BF16)

    def head_softmax_pv(qb, a, s, gate):
        s = list(s)
        if gate is not None:
            for n in range(qb):
                g_n = gate[n:n + 1, :]
                beats = ((gate > g_n) | ((gate == g_n) & (cand < n))) & (cand < qb)
                rank = jnp.sum(jnp.where(beats, 1.0, 0.0), axis=0, keepdims=True)
                unselected = jnp.where(rank < MOBA_TOPK, 0.0, NEG_INF)
                for h in range(per_block):
                    s[per_block * n + h] = s[per_block * n + h] + unselected
        for h in range(per_block):
            s[per_block * qb + h] = s[per_block * qb + h] + causal[h]
        m = _reduce_rows(s, jnp.maximum, jnp.max)
        p = jnp.concatenate([jnp.exp2(x - m).astype(BF16) for x in s], axis=0)
        acc = _dot(vt_ref[a, :, 0:(qb + 1) * blk], p)
        return acc[:HEAD_DIM] * (1.0 / acc[HEAD_DIM:HEAD_DIM + 1])

    units = [(qb, hp) for qb in range(n_blocks) for hp in range(n_pairs)]
    _skewed(units, (load_q, scores, softmax_pv, store), (0, 1, 2, 3))


def _moba_call(qkv, bsz):
    t, n = qkv.shape
    seq = t // bsz
    d = n // 3
    cw = MOBA_HEAD_PAIRS * LANES
    col_blocks = d // cw
    return pl.pallas_call(
        _moba_kernel,
        grid=(bsz, col_blocks),
        in_specs=[pl.BlockSpec((seq, cw), lambda b, c: (b, c)),
                  pl.BlockSpec((seq, cw), lambda b, c: (b, col_blocks + c)),
                  pl.BlockSpec((seq, cw), lambda b, c: (b, 2 * col_blocks + c))],
        out_specs=pl.BlockSpec((seq, cw), lambda b, c: (b, c)),
        out_shape=jax.ShapeDtypeStruct((t, d), BF16),
        scratch_shapes=[pltpu.VMEM((2 * MOBA_HEAD_PAIRS, HEAD_DIM + ONES_ROWS, seq), BF16)],
        compiler_params=_params(("arbitrary", "arbitrary"), ((seq, cw), BF16, 8),
                                ((2 * MOBA_HEAD_PAIRS, HEAD_DIM + ONES_ROWS, seq), BF16, 1)),
        name="moba",
    )(qkv, qkv, qkv)


def _band_kernel(q_ref, k_ref, v_ref, o_ref, lse_ref, *, blocks_per_seq):
    w = BAND
    n_blocks = q_ref.shape[0] // w
    n_pairs = q_ref.shape[1] // LANES
    key_i = lax.broadcasted_iota(jnp.int32, (w, 2 * w), 0)
    qry_i = lax.broadcasted_iota(jnp.int32, (w, 2 * w), 1) % w
    bias_of = {0: jnp.where(key_i <= qry_i, 0.0, NEG_INF), 1: jnp.where(key_i >= qry_i, 0.0, NEG_INF)}
    dim = lax.broadcasted_iota(jnp.int32, (LANES, w), 0)
    rows = lambda j: slice(j * w, (j + 1) * w)
    cols = lambda hp: slice(hp * LANES, (hp + 1) * LANES)

    def back_blocks(j):
        return (0,) if j % blocks_per_seq == 0 else (0, 1)

    v_t = {}

    def transposes(unit, _):
        j, hp = unit
        v_t[unit] = v_ref[rows(j), cols(hp)].astype(F32).T.astype(BF16)
        q_t = q_ref[rows(j), cols(hp)].astype(F32).T
        return jnp.concatenate([jnp.where(_in_head(dim, a), q_t, 0.0) for a in range(2)],
                               axis=1).astype(BF16)

    def scores(unit, q2):
        j, hp = unit
        return [_dot(k_ref[rows(j - back), cols(hp)], q2) for back in back_blocks(j)]

    def softmax_pv(unit, s_blocks):
        j, hp = unit
        s_blocks = [s + bias_of[back] for s, back in zip(s_blocks, back_blocks(j))]
        m = _reduce_rows(s_blocks, jnp.maximum, jnp.max)
        p_blocks = [jnp.exp2(s - m) for s in s_blocks]
        l = _reduce_rows(p_blocks, jnp.add, jnp.sum)
        o_t = functools.reduce(jnp.add, [_dot(v_t[j - back, hp], p.astype(BF16))
                                         for p, back in zip(p_blocks, back_blocks(j))]) * (1.0 / l)
        if j % blocks_per_seq != 0:
            del v_t[j - 1, hp]
        if (j + 1) % blocks_per_seq == 0:
            del v_t[unit]
        lse = (m + jnp.log2(l)) * LN2
        lse_ref[0, 2 * hp:2 * hp + 1, rows(j)] = lse[:, :w]
        lse_ref[0, 2 * hp + 1:2 * hp + 2, rows(j)] = lse[:, w:]
        return jnp.concatenate([o_t[:HEAD_DIM, :w], o_t[HEAD_DIM:, w:]], axis=0)

    def store(unit, o_sel):
        j, hp = unit
        o_ref[rows(j), cols(hp)] = o_sel.T.astype(BF16)

    units = [(j, hp) for j in range(n_blocks) for hp in range(n_pairs)]
    _skewed(units, (transposes, scores, softmax_pv, store), (0, 3, 6, 9))


def _band_call(q, k, v, dil, bsz):
    t, d = q.shape
    seq = t // bsz
    cw = BAND_HEAD_PAIRS * LANES
    blocks_per_seq = (seq // dil) // BAND
    blk = pl.BlockSpec((seq, cw), lambda b, c: (b, c))
    return pl.pallas_call(
        functools.partial(_band_kernel, blocks_per_seq=blocks_per_seq),
        grid=(bsz, d // cw),
        in_specs=[blk, blk, blk],
        out_specs=[blk, pl.BlockSpec((1, 2 * BAND_HEAD_PAIRS, seq), lambda b, c: (b, c, 0))],
        out_shape=[jax.ShapeDtypeStruct((t, d), BF16),
                   jax.ShapeDtypeStruct((bsz, N_HEADS, seq), F32)],
        compiler_params=_params(("arbitrary", "arbitrary"), ((seq, cw), BF16, 8),
                                ((2 * BAND_HEAD_PAIRS, seq), F32, 2)),
        name="band%d" % dil,
    )(q, k, v)


def _lse_token_order(lse, dil):
    bsz, n_h, seq = lse.shape
    return lse.reshape(bsz, n_h, dil, seq // dil).transpose(0, 3, 2, 1).reshape(bsz * seq, n_h)


def _rope_tables(seq):
    inv = 1.0 / (ROPE_THETA ** (jnp.arange(0, HEAD_DIM, 2, dtype=F32) / HEAD_DIM))
    ang = jnp.arange(seq, dtype=F32)[:, None] * inv[None, :]
    cos, sin = jnp.cos(ang), jnp.sin(ang)
    tabs = (jnp.concatenate([cos, cos, cos, cos], axis=1),
            jnp.concatenate([-sin, -sin, sin, sin], axis=1))

    def strided(tab, dil):
        tiles = seq // TOKEN_TILE
        return tab.reshape(tiles, TOKEN_TILE // dil, dil, LANES).transpose(0, 2, 1, 3).reshape(seq, LANES)

    return tuple(jnp.stack([strided(tab, dil) for dil in DILATIONS]) for tab in tabs)


def _prep_kernel(w_ref, perm_ref, o_ref, *, rope):
    for s, roped in enumerate(rope):
        for c in range(D_MODEL // LANES):
            sl = slice(s * D_MODEL + c * LANES, s * D_MODEL + (c + 1) * LANES)
            chunk = w_ref[:, sl].astype(BF16)
            o_ref[:, sl] = _dot(chunk, perm_ref[...]).astype(BF16) if roped else chunk


def _pair_split_matrix():
    half = HEAD_DIM // 2
    dst = jnp.arange(LANES)
    group, r = dst // half, dst % half
    src = (group % 2) * HEAD_DIM + (group // 2) * half + r
    return (jnp.arange(LANES)[:, None] == src[None, :]).astype(BF16)


def _prep_call(w, sections):
    n_l, d, n = w.shape
    rows = 256
    blk = pl.BlockSpec((None, rows, n), lambda l, i: (l, i, 0))
    return pl.pallas_call(
        functools.partial(_prep_kernel, rope=tuple(rope for rope, _, _ in sections)),
        grid=(n_l, d // rows),
        in_specs=[blk, pl.BlockSpec((LANES, LANES), lambda l, i: (0, 0))],
        out_specs=blk,
        out_shape=jax.ShapeDtypeStruct((n_l, d, n), BF16),
        compiler_params=_params(("arbitrary", "arbitrary"), ((rows, n), F32, 2), ((rows, n), BF16, 2),
                                ((LANES, LANES), BF16, 2)),
        name="prep%d" % n,
    )(w, _pair_split_matrix())


def _vec(bsz, *rows):
    rows = [jnp.broadcast_to(r, (bsz, D_MODEL)) for r in rows]
    rows += [jnp.zeros((bsz, D_MODEL), F32)] * (8 - len(rows))
    return jnp.stack(rows, axis=1)


def kernel(x, c, ada_w, ada_b, norm_g, ffn_w_gate, ffn_w_up, ffn_w_down, moba_w_qkv, moba_w_o,
           kv_ada_w, kv_ada_b, kv_norm_g, kv_w, dil_w_q, dil_w_o, final_g):
    bsz, seq, d = x.shape
    t = bsz * seq
    tables = _rope_tables(seq)
    natural_tables = tuple(tab[:1] for tab in tables)
    mod = _mod_call(c, ada_w, ada_b).reshape(DEPTH, bsz, 3, 3, d)
    kv_mod = _mod_call(c, kv_ada_w[None], kv_ada_b[None]).reshape(bsz, 2, d)
    head_expand = jnp.tile(jnp.repeat(jnp.eye(N_HEADS, dtype=BF16), HEAD_DIM, axis=1), (2, 1))
    q_scale = HEAD_DIM ** -0.5 * LOG2E

    ffn_w = (ffn_w_gate, ffn_w_up, ffn_w_down.astype(BF16))
    moba_wo, dil_wo = moba_w_o.astype(BF16), dil_w_o.astype(BF16)
    qkv_sections = ((True, q_scale, 1), (True, 1.0, 1), (False, 1.0, 1))
    q_sections = tuple((True, q_scale, dil) for dil in DILATIONS)
    kv_sections = tuple((is_k, 1.0, dil) for dil in DILATIONS for is_k in (True, False))
    moba_wqkv = _prep_call(moba_w_qkv, qkv_sections)
    dil_wq = _prep_call(dil_w_q, q_sections)
    kv_wp = _prep_call(kv_w[None], kv_sections)

    xf = x.reshape(t, d)
    kvs = None
    for layer in range(DEPTH):
        m = mod[layer]
        if layer == N_A_LAYERS:
            kvs = _proj_call(xf, _vec(bsz, kv_norm_g, kv_mod[:, 0], kv_mod[:, 1]), tables, kv_wp, 0,
                             sections=kv_sections, fused_out=False, bsz=bsz)
        xf = _ffn_call(xf, _vec(bsz, norm_g[layer, 0], m[:, 0, 0], m[:, 0, 1], m[:, 0, 2]),
                       ffn_w, layer, 0)

        mix_vec = _vec(bsz, norm_g[layer, 1], m[:, 1, 0], m[:, 1, 1])
        if layer < N_A_LAYERS:
            lb = layer
            qkv, = _proj_call(xf, mix_vec, natural_tables, moba_wqkv, lb,
                              sections=qkv_sections, fused_out=True, bsz=bsz)
            mixer = (_moba_call(qkv, bsz), moba_wo)
            mode = "moba"
        else:
            lb = layer - N_A_LAYERS
            qs = _proj_call(xf, mix_vec, tables, dil_wq, lb,
                            sections=q_sections, fused_out=False, bsz=bsz)
            outs, lses = [], []
            for g, (window, dil) in enumerate(DILATED_BRANCHES):
                assert window // dil == BAND
                o_g, lse_g = _band_call(qs[g].reshape(t, d), kvs[2 * g].reshape(t, d),
                                        kvs[2 * g + 1].reshape(t, d), dil, bsz)
                outs.append(o_g if dil == 1 else o_g.reshape(bsz, dil, seq // dil, d))
                lses.append(_lse_token_order(lse_g, dil))
            mixer = (*outs, *lses, head_expand, dil_wo)
            mode = "dil"
        xf = _ffn_call(xf, _vec(bsz, norm_g[layer, 2], m[:, 2, 0], m[:, 2, 1], m[:, 2, 2],
                                m[:, 1, 2], final_g),
                       ffn_w, layer, 1, mode=mode, final=(layer == DEPTH - 1), mixer=mixer, mixer_layer=lb)
    return xf.reshape(bsz, seq, d)
```

```python
import functools
import math

import jax
import jax.numpy as jnp
from jax import lax
from jax.experimental import pallas as pl
from jax.experimental.pallas import tpu as pltpu

F32 = jnp.float32
BF16 = jnp.bfloat16

D_MODEL = 1024
HEAD_DIM = 64
N_HEADS = D_MODEL // HEAD_DIM
D_FF = 2816
ROPE_THETA = 10000.0
RMS_EPS = 1e-6
DEPTH = 4
N_A_LAYERS = 2
MOBA_BLOCK = 256
MOBA_TOPK = 3
MOBA_PIECE = 128
ONES_ROWS = 16
MOBA_HEAD_PAIRS = 2
DILATED_BRANCHES = ((128, 1), (512, 4), (2048, 16))
N_BRANCHES = len(DILATED_BRANCHES)
DILATIONS = tuple(d for _, d in DILATED_BRANCHES)
BAND = 128
LANES = 128
SUBLANE_STRIDE = 4
FFN_CHUNK = 256
TOKEN_TILE = 512
FFN_TILE = 512
BAND_HEAD_PAIRS = 4
NEG_INF = float("-inf")
LOG2E = math.log2(math.e)
LN2 = math.log(2.0)
MIB = 1024 * 1024
V7X_VMEM_BYTES = 64 * MIB


def _dot(a, b):
    return jnp.dot(a, b, preferred_element_type=F32)


def _split_bf16(a):
    hi = a.astype(BF16)
    lo = (a - hi.astype(F32)).astype(BF16)
    return hi, lo


def _resident(shape, lead=()):
    return pl.BlockSpec((None,) * len(lead) + tuple(shape), lambda *_: tuple(lead) + (0,) * len(shape),
                        pipeline_mode=pl.Buffered(1))


def _fold_rows(x, op):
    while x.shape[0] > 8 and x.shape[0] % 16 == 0:
        half = x.shape[0] // 2
        x = op(x[:half], x[half:])
    return x


def _reduce_rows(pieces, op, reduce_fn):
    folded = functools.reduce(op, [_fold_rows(x, op) for x in pieces])
    return reduce_fn(folded, axis=0, keepdims=True)


def _params(semantics, *buffers):
    declared = sum(math.prod(shape) * jnp.dtype(dtype).itemsize * count for shape, dtype, count in buffers)
    limit = min(declared + V7X_VMEM_BYTES // 4, V7X_VMEM_BYTES - 8 * MIB)
    assert declared < limit, (declared, limit)
    return pltpu.CompilerParams(dimension_semantics=semantics, vmem_limit_bytes=limit)


def _skewed(units, stages, delays):
    results = {}
    for t in range(len(units) + delays[-1]):
        for k, (stage, delay) in enumerate(zip(stages, delays)):
            i = t - delay
            if 0 <= i < len(units):
                results[k, i] = stage(units[i], results.pop((k - 1, i), None))


def _mod_kernel(c_ref, w_ref, b_ref, o_ref):
    c = c_ref[...]
    a_hi, a_lo = _split_bf16(c * jax.nn.sigmoid(c))
    w_hi, w_lo = _split_bf16(w_ref[0])
    n_b = c.shape[0]
    both = _dot(jnp.concatenate([a_hi, a_lo], axis=0), w_hi)
    o_ref[0] = both[:n_b] + both[n_b:] + _dot(a_hi, w_lo) + b_ref[0]


def _mod_call(c, w, b):
    n_l, d, n = w.shape
    bsz = c.shape[0]
    tn = 1024
    return pl.pallas_call(
        _mod_kernel,
        grid=(n_l, n // tn),
        in_specs=[pl.BlockSpec((bsz, d), lambda l, j: (0, 0)),
                  pl.BlockSpec((1, d, tn), lambda l, j: (l, 0, j)),
                  pl.BlockSpec((1, 1, tn), lambda l, j: (l, 0, j))],
        out_specs=pl.BlockSpec((1, bsz, tn), lambda l, j: (l, 0, j)),
        out_shape=jax.ShapeDtypeStruct((n_l, bsz, n), F32),
        compiler_params=_params(("arbitrary", "arbitrary"), ((bsz, d), F32, 2), ((d, tn), F32, 2),
                                ((1, tn), F32, 2), ((bsz, tn), F32, 2)),
        name="mod",
    )(c, w, b.reshape(n_l, 1, n))


def _norm_mod(x, g, shift, scale):
    ms = jnp.mean(x * x, axis=-1, keepdims=True)
    return (x * lax.rsqrt(ms + RMS_EPS) * g) * (1.0 + scale) + shift


def _store_lane_chunks(ref3, x):
    for c in range(ref3.shape[0]):
        ref3[c] = x[:, c * LANES:(c + 1) * LANES]


def _load_lane_chunks(ref3):
    return jnp.concatenate([ref3[c] for c in range(ref3.shape[0])], axis=1)


def _load_strided_order(ref3, dil, tmp3=None, coarse=None):
    n_rows = ref3.shape[1]
    if dil <= SUBLANE_STRIDE:
        cols = [jnp.concatenate([ref3[c, pl.ds(r, n_rows // dil, stride=dil), :] for r in range(dil)], axis=0)
                for c in range(ref3.shape[0])]
        return jnp.concatenate(cols, axis=1)
    f = SUBLANE_STRIDE
    assert dil == f * f
    _store_lane_chunks(tmp3, _load_strided_order(ref3, f) if coarse is None else coarse)
    part = n_rows // f
    cols = [jnp.concatenate([tmp3[c, pl.ds((r % f) * part + r // f, n_rows // dil, stride=f), :]
                             for r in range(dil)], axis=0)
            for c in range(ref3.shape[0])]
    return jnp.concatenate(cols, axis=1)


def _store_token_order(ref3, block_ref, dil, tmp3=None):
    rows = block_ref.shape[2]
    f = SUBLANE_STRIDE
    two_pass = dil > f
    assert not two_pass or dil == f * f
    part = ref3.shape[1] // f
    for r in range(dil):
        piece = block_ref[0, r].astype(F32)
        for c in range(ref3.shape[0]):
            chunk = piece[:, c * LANES:(c + 1) * LANES]
            if two_pass:
                tmp3[c, pl.ds((r % f) * part + r // f, rows, stride=f), :] = chunk
            else:
                ref3[c, pl.ds(r, rows, stride=dil), :] = chunk
    if two_pass:
        for b in range(f):
            for c in range(ref3.shape[0]):
                ref3[c, pl.ds(b, part, stride=f), :] = tmp3[c, b * part:(b + 1) * part, :]


def _ffn_kernel(*refs, mode, final):
    it = iter(refs)
    x_ref, vec_ref = next(it), next(it)
    x = x_ref[...]
    vec = vec_ref[0]
    if mode == "moba":
        attn_ref, wo_ref = next(it), next(it)
        x = x + vec[4:5] * _dot(attn_ref[...], wo_ref[...])
    elif mode == "dil":
        o_refs = [next(it) for _ in range(N_BRANCHES)]
        l_refs = [next(it) for _ in range(N_BRANCHES)]
        e_ref, wo_ref = next(it), next(it)
    wg_ref, wu_ref, wd_ref = next(it), next(it), next(it)
    o_ref, a_ref = next(it), next(it)
    if mode == "dil":
        u_ref, tmp_ref = next(it), next(it)
        lse = [r[...] for r in l_refs]
        mx = functools.reduce(jnp.maximum, lse)
        ex = [jnp.exp(l - mx) for l in lse]
        inv = 1.0 / functools.reduce(lambda a, b: a + b, ex)
        attn = None
        for e, br_ref, dil in zip(ex, o_refs, DILATIONS):
            if dil == 1:
                o_g = br_ref[...].astype(F32)
            else:
                _store_token_order(u_ref, br_ref, dil, tmp_ref)
                o_g = _load_lane_chunks(u_ref)
            w_full = _dot(jnp.concatenate(_split_bf16(e * inv), axis=1), e_ref[...])
            term = w_full * o_g
            attn = term if attn is None else attn + term
        x = x + vec[4:5] * _dot(attn.astype(BF16), wo_ref[...])

    h = _norm_mod(x, vec[0:1], vec[1:2], vec[2:3]).astype(BF16)
    for c in range(D_FF // FFN_CHUNK):
        sl = slice(c * FFN_CHUNK, (c + 1) * FFN_CHUNK)
        gate = _dot(h, wg_ref[:, sl].astype(BF16))
        up = _dot(h, wu_ref[:, sl].astype(BF16))
        a_ref[:, sl] = (gate * jax.nn.sigmoid(gate) * up).astype(BF16)
    y = x + (0.5 * vec[3:4]) * _dot(a_ref[...], wd_ref[...])
    if final:
        ms = jnp.mean(y * y, axis=-1, keepdims=True)
        y = y * lax.rsqrt(ms + RMS_EPS) * vec[5:6]
    o_ref[...] = y


def _strided_block_spec(dil, tm, tiles_per_seq):
    return pl.BlockSpec((1, dil, tm // dil, D_MODEL),
                        lambda i: (i // tiles_per_seq, 0, i % tiles_per_seq, 0))


def _ffn_call(x, vec, ffn_w, layer, half, *, mode="none", final=False, mixer=(), mixer_layer=0):
    t, d = x.shape
    bsz = vec.shape[0]
    tm = TOKEN_TILE if mode == "dil" else FFN_TILE
    tiles_per_seq = (t // bsz) // tm
    row = lambda i: (i, 0)
    in_specs = [pl.BlockSpec((tm, d), row),
                pl.BlockSpec((1, 8, d), lambda i: (i // tiles_per_seq, 0, 0))]
    scratch = [pltpu.VMEM((tm, D_FF), BF16)]
    buffers = [((tm, d), F32, 4), ((8, d), F32, 2), ((tm, D_FF), BF16, 1),
               ((d, D_FF), F32, 2), ((D_FF, d), BF16, 1)]
    if mode == "moba":
        in_specs += [pl.BlockSpec((tm, d), row), _resident((d, d), (mixer_layer,))]
        buffers += [((tm, d), BF16, 2), ((d, d), BF16, 1)]
    elif mode == "dil":
        for dil in DILATIONS:
            in_specs.append(pl.BlockSpec((tm, d), row) if dil == 1
                            else _strided_block_spec(dil, tm, tiles_per_seq))
        in_specs += [pl.BlockSpec((tm, N_HEADS), row)] * N_BRANCHES
        in_specs += [_resident((2 * N_HEADS, d)), _resident((d, d), (mixer_layer,))]
        scratch += [pltpu.VMEM((d // LANES, tm, LANES), F32)] * 2
        buffers += [((tm, d), BF16, 2 * N_BRANCHES), ((tm, LANES), F32, 2 * N_BRANCHES),
                    ((2 * N_HEADS, d), BF16, 1), ((d, d), BF16, 1), ((tm, d), F32, 2)]
    which = (layer, half)
    in_specs += [_resident((d, D_FF), which), _resident((d, D_FF), which), _resident((D_FF, d), which)]
    return pl.pallas_call(
        functools.partial(_ffn_kernel, mode=mode, final=final),
        grid=(t // tm,),
        in_specs=in_specs,
        out_specs=pl.BlockSpec((tm, d), row),
        out_shape=jax.ShapeDtypeStruct((t, d), F32),
        scratch_shapes=scratch,
        compiler_params=_params(("arbitrary",), *buffers),
        name="ffn_" + mode,
    )(x, vec, *mixer, *ffn_w)


def _in_head(dim, a):
    return (dim // (HEAD_DIM // 2)) % 2 == a


def _rope(y, cos, sin):
    pieces = []
    for c in range(y.shape[1] // LANES):
        yc = y[:, c * LANES:(c + 1) * LANES]
        pieces.append(yc * cos + pltpu.roll(yc, LANES // 2, 1) * sin)
    return jnp.concatenate(pieces, axis=1)


def _proj_kernel(x_ref, vec_ref, cos_ref, sin_ref, w_ref, *rest, sections, fused_out):
    o_refs, h_ref, tmp_ref = rest[:-2], rest[-2], rest[-1]
    tm = x_ref.shape[0]
    vec = vec_ref[0]
    h = _norm_mod(x_ref[...], vec[0:1], vec[1:2], vec[2:3])
    dils = sorted({dil for _, _, dil in sections})
    if dils != [1]:
        _store_lane_chunks(h_ref, h)
    h_f32 = {1: h}
    for dil in dils[1:]:
        h_f32[dil] = _load_strided_order(h_ref, dil, tmp_ref, h_f32.get(SUBLANE_STRIDE))
    h_by_dil = {dil: h_f32[dil].astype(BF16) for dil in dils}
    for s, (rope, scale, dil) in enumerate(sections):
        y = _dot(h_by_dil[dil], w_ref[:, s * D_MODEL:(s + 1) * D_MODEL])
        if rope:
            t_i = dils.index(dil)
            y = _rope(y, cos_ref[t_i], sin_ref[t_i])
        if scale != 1.0:
            y = y * scale
        y = y.astype(BF16)
        if fused_out:
            o_refs[0][:, s * D_MODEL:(s + 1) * D_MODEL] = y
        elif dil == 1:
            o_refs[s][...] = y
        else:
            rows = tm // dil
            for r in range(dil):
                o_refs[s][0, r] = y[r * rows:(r + 1) * rows, :]


def _proj_call(x, vec, tables, w, layer, *, sections, fused_out, bsz):
    t, d = x.shape
    n = w.shape[2]
    tm = TOKEN_TILE
    seq = t // bsz
    tiles_per_seq = seq // tm
    row = lambda i: (i, 0)
    n_tab = tables[0].shape[0]
    tab_spec = pl.BlockSpec((n_tab, tm, LANES), lambda i: (0, i % tiles_per_seq, 0))
    if fused_out:
        out_specs = [pl.BlockSpec((tm, n), row)]
        out_shape = [jax.ShapeDtypeStruct((t, n), BF16)]
    else:
        out_specs, out_shape = [], []
        for _, _, dil in sections:
            if dil == 1:
                out_specs.append(pl.BlockSpec((tm, d), row))
                out_shape.append(jax.ShapeDtypeStruct((t, d), BF16))
            else:
                out_specs.append(_strided_block_spec(dil, tm, tiles_per_seq))
                out_shape.append(jax.ShapeDtypeStruct((bsz, dil, seq // dil, d), BF16))
    return pl.pallas_call(
        functools.partial(_proj_kernel, sections=sections, fused_out=fused_out),
        grid=(t // tm,),
        in_specs=[pl.BlockSpec((tm, d), row),
                  pl.BlockSpec((1, 8, d), lambda i: (i // tiles_per_seq, 0, 0)),
                  tab_spec, tab_spec,
                  _resident((d, n), (layer,))],
        out_specs=out_specs,
        out_shape=out_shape,
        scratch_shapes=[pltpu.VMEM((d // LANES, tm, LANES), F32)] * 2,
        compiler_params=_params(("arbitrary",), ((tm, d), F32, 3), ((8, d), F32, 2),
                                ((n_tab, tm, LANES), F32, 4), ((d, n), BF16, 1), ((tm, n), BF16, 2), ((tm, d), F32, 1)),
        name="proj%d" % n,
    )(x, vec, *tables, w)


def _moba_kernel(q_ref, k_ref, v_ref, o_ref, vt_ref):
    blk = MOBA_BLOCK
    n_blocks = q_ref.shape[0] // blk
    n_pairs = q_ref.shape[1] // LANES
    rows = lambda n: slice(n * blk, (n + 1) * blk)
    cols = lambda hp: slice(hp * LANES, (hp + 1) * LANES)

    k_mean = []
    for hp in range(n_pairs):
        block_means = []
        for a in range(2):
            vt_ref[2 * hp + a, HEAD_DIM:, :] = jnp.ones((ONES_ROWS, vt_ref.shape[2]), BF16)
        for n in range(n_blocks):
            v_t = v_ref[rows(n), cols(hp)].astype(F32).T.astype(BF16)
            for a in range(2):
                vt_ref[2 * hp + a, :HEAD_DIM, rows(n)] = v_t[a * HEAD_DIM:(a + 1) * HEAD_DIM, :]
            block_means.append(jnp.mean(k_ref[rows(n), cols(hp)].astype(F32), axis=0, keepdims=True))
        k_mean.append(_split_bf16(jnp.concatenate(block_means, axis=0)))

    dim = lax.broadcasted_iota(jnp.int32, (LANES, blk), 0)
    in_head = [_in_head(dim, a) for a in range(2)]
    cand = lax.broadcasted_iota(jnp.int32, (n_blocks, blk), 0)
    piece = MOBA_PIECE
    per_block = blk // piece
    key_i = lax.broadcasted_iota(jnp.int32, (piece, piece), 0)
    qry_i = lax.broadcasted_iota(jnp.int32, (piece, piece), 1)
    causal_tile = jnp.where(key_i <= qry_i, 0.0, NEG_INF)

    def load_q(unit, _):
        qb, hp = unit
        q_t = q_ref[rows(qb), cols(hp)].astype(F32).T
        return [jnp.where(in_head[a], q_t, 0.0).astype(BF16) for a in range(2)]

    def scores(unit, q_heads):
        qb, hp = unit
        km_hi, km_lo = k_mean[hp]
        out = []
        for q_a in q_heads:
            s_all = _dot(k_ref[0:(qb + 1) * blk, cols(hp)], q_a)
            s = [s_all[i * piece:(i + 1) * piece, :] for i in range(per_block * (qb + 1))]
            gate = _dot(km_hi, q_a) + _dot(km_lo, q_a) if qb > MOBA_TOPK else None
            out.append((s, gate))
        return out

    def softmax_pv(unit, stage):
        qb, hp = unit
        return [head_softmax_pv(qb, 2 * hp + a, *stage[a]) for a in range(2)]

    def store(unit, heads):
        qb, hp = unit
        o_ref[rows(qb), cols(hp)] = jnp.concatenate(heads, axis=0).T.astype(BF16)

    def head_softmax_pv(qb, a, s, gate):
        unselected = {}
        if gate is not None:
            for n in range(qb):
                g_n = gate[n:n + 1, :]
                beats = ((gate > g_n) | ((gate == g_n) & (cand < n))) & (cand < qb)
                rank = jnp.sum(jnp.where(beats, 1.0, 0.0), axis=0, keepdims=True)
                unselected[n] = jnp.where(rank < MOBA_TOPK, 0.0, NEG_INF)
        assert piece * 2 == blk
        lane_halves = [slice(0, piece), slice(piece, blk)]
        columns = [[], []]
        for i, x in enumerate(s):
            n, h = divmod(i, per_block)
            for j, lanes in enumerate(lane_halves):
                tile = x[:, lanes]
                if n == qb:
                    tile = None if h > j else (tile + causal_tile if h == j else tile)
                elif n in unselected:
                    tile = tile + unselected[n][:, lanes]
                columns[j].append(tile)
        p_columns = []
        for column in columns:
            m = _reduce_rows([t for t in column if t is not None], jnp.maximum, jnp.max)
            p_columns.append([jnp.zeros((piece, piece), BF16) if t is None else jnp.exp2(t - m).astype(BF16)
                              for t in column])
        p = jnp.concatenate([jnp.concatenate([p_columns[0][i], p_columns[1][i]], axis=1)
                             for i in range(len(s))], axis=0)
        acc = _dot(vt_ref[a, :, 0:(qb + 1) * blk], p)
        return acc[:HEAD_DIM] * (1.0 / acc[HEAD_DIM:HEAD_DIM + 1])

    units = [(qb, hp) for qb in range(n_blocks) for hp in range(n_pairs)]
    _skewed(units, (load_q, scores, softmax_pv, store), (0, 1, 2, 3))


def _moba_call(qkv, bsz):
    t, n = qkv.shape
    seq = t // bsz
    d = n // 3
    cw = MOBA_HEAD_PAIRS * LANES
    col_blocks = d // cw
    return pl.pallas_call(
        _moba_kernel,
        grid=(bsz, col_blocks),
        in_specs=[pl.BlockSpec((seq, cw), lambda b, c: (b, c)),
                  pl.BlockSpec((seq, cw), lambda b, c: (b, col_blocks + c)),
                  pl.BlockSpec((seq, cw), lambda b, c: (b, 2 * col_blocks + c))],
        out_specs=pl.BlockSpec((seq, cw), lambda b, c: (b, c)),
        out_shape=jax.ShapeDtypeStruct((t, d), BF16),
        scratch_shapes=[pltpu.VMEM((2 * MOBA_HEAD_PAIRS, HEAD_DIM + ONES_ROWS, seq), BF16)],
        compiler_params=_params(("arbitrary", "arbitrary"), ((seq, cw), BF16, 8),
                                ((2 * MOBA_HEAD_PAIRS, HEAD_DIM + ONES_ROWS, seq), BF16, 1)),
        name="moba",
    )(qkv, qkv, qkv)


def _band_kernel(q_ref, k_ref, v_ref, o_ref, lse_ref, *, blocks_per_seq):
    w = BAND
    n_blocks = q_ref.shape[0] // w
    n_pairs = q_ref.shape[1] // LANES
    key_i = lax.broadcasted_iota(jnp.int32, (w, 2 * w), 0)
    qry_i = lax.broadcasted_iota(jnp.int32, (w, 2 * w), 1) % w
    bias_of = {0: jnp.where(key_i <= qry_i, 0.0, NEG_INF), 1: jnp.where(key_i >= qry_i, 0.0, NEG_INF)}
    dim = lax.broadcasted_iota(jnp.int32, (LANES, w), 0)
    rows = lambda j: slice(j * w, (j + 1) * w)
    cols = lambda hp: slice(hp * LANES, (hp + 1) * LANES)

    def back_blocks(j):
        return (0,) if j % blocks_per_seq == 0 else (0, 1)

    v_t = {}

    def transposes(unit, _):
        j, hp = unit
        v_t[unit] = v_ref[rows(j), cols(hp)].astype(F32).T.astype(BF16)
        q_t = q_ref[rows(j), cols(hp)].astype(F32).T
        return jnp.concatenate([jnp.where(_in_head(dim, a), q_t, 0.0) for a in range(2)],
                               axis=1).astype(BF16)

    def scores(unit, q2):
        j, hp = unit
        return [_dot(k_ref[rows(j - back), cols(hp)], q2) for back in back_blocks(j)]

    def softmax_pv(unit, s_blocks):
        j, hp = unit
        s_blocks = [s + bias_of[back] for s, back in zip(s_blocks, back_blocks(j))]
        m = _reduce_rows(s_blocks, jnp.maximum, jnp.max)
        p_blocks = [jnp.exp2(s - m) for s in s_blocks]
        l = _reduce_rows(p_blocks, jnp.add, jnp.sum)
        o_t = functools.reduce(jnp.add, [_dot(v_t[j - back, hp], p.astype(BF16))
                                         for p, back in zip(p_blocks, back_blocks(j))]) * (1.0 / l)
        if j % blocks_per_seq != 0:
            del v_t[j - 1, hp]
        if (j + 1) % blocks_per_seq == 0:
            del v_t[unit]
        lse = (m + jnp.log2(l)) * LN2
        lse_ref[0, 2 * hp:2 * hp + 1, rows(j)] = lse[:, :w]
        lse_ref[0, 2 * hp + 1:2 * hp + 2, rows(j)] = lse[:, w:]
        return jnp.concatenate([o_t[:HEAD_DIM, :w], o_t[HEAD_DIM:, w:]], axis=0)

    def store(unit, o_sel):
        j, hp = unit
        o_ref[rows(j), cols(hp)] = o_sel.T.astype(BF16)

    units = [(j, hp) for j in range(n_blocks) for hp in range(n_pairs)]
    _skewed(units, (transposes, scores, softmax_pv, store), (0, 3, 6, 9))


def _band_call(q, k, v, dil, bsz):
    t, d = q.shape
    seq = t // bsz
    cw = BAND_HEAD_PAIRS * LANES
    blocks_per_seq = (seq // dil) // BAND
    blk = pl.BlockSpec((seq, cw), lambda b, c: (b, c))
    return pl.pallas_call(
        functools.partial(_band_kernel, blocks_per_seq=blocks_per_seq),
        grid=(bsz, d // cw),
        in_specs=[blk, blk, blk],
        out_specs=[blk, pl.BlockSpec((1, 2 * BAND_HEAD_PAIRS, seq), lambda b, c: (b, c, 0))],
        out_shape=[jax.ShapeDtypeStruct((t, d), BF16),
                   jax.ShapeDtypeStruct((bsz, N_HEADS, seq), F32)],
        compiler_params=_params(("arbitrary", "arbitrary"), ((seq, cw), BF16, 8),
                                ((2 * BAND_HEAD_PAIRS, seq), F32, 2)),
        name="band%d" % dil,
    )(q, k, v)


def _lse_token_order(lse, dil):
    bsz, n_h, seq = lse.shape
    return lse.reshape(bsz, n_h, dil, seq // dil).transpose(0, 3, 2, 1).reshape(bsz * seq, n_h)


def _rope_tables(seq):
    inv = 1.0 / (ROPE_THETA ** (jnp.arange(0, HEAD_DIM, 2, dtype=F32) / HEAD_DIM))
    ang = jnp.arange(seq, dtype=F32)[:, None] * inv[None, :]
    cos, sin = jnp.cos(ang), jnp.sin(ang)
    tabs = (jnp.concatenate([cos, cos, cos, cos], axis=1),
            jnp.concatenate([-sin, -sin, sin, sin], axis=1))

    def strided(tab, dil):
        tiles = seq // TOKEN_TILE
        return tab.reshape(tiles, TOKEN_TILE // dil, dil, LANES).transpose(0, 2, 1, 3).reshape(seq, LANES)

    return tuple(jnp.stack([strided(tab, dil) for dil in DILATIONS]) for tab in tabs)


def _prep_kernel(w_ref, perm_ref, o_ref, *, rope):
    for s, roped in enumerate(rope):
        for c in range(D_MODEL // LANES):
            sl = slice(s * D_MODEL + c * LANES, s * D_MODEL + (c + 1) * LANES)
            chunk = w_ref[:, sl].astype(BF16)
            o_ref[:, sl] = _dot(chunk, perm_ref[...]).astype(BF16) if roped else chunk


def _pair_split_matrix():
    half = HEAD_DIM // 2
    dst = jnp.arange(LANES)
    group, r = dst // half, dst % half
    src = (group % 2) * HEAD_DIM + (group // 2) * half + r
    return (jnp.arange(LANES)[:, None] == src[None, :]).astype(BF16)


def _prep_call(w, sections):
    n_l, d, n = w.shape
    rows = 256
    blk = pl.BlockSpec((None, rows, n), lambda l, i: (l, i, 0))
    return pl.pallas_call(
        functools.partial(_prep_kernel, rope=tuple(rope for rope, _, _ in sections)),
        grid=(n_l, d // rows),
        in_specs=[blk, pl.BlockSpec((LANES, LANES), lambda l, i: (0, 0))],
        out_specs=blk,
        out_shape=jax.ShapeDtypeStruct((n_l, d, n), BF16),
        compiler_params=_params(("arbitrary", "arbitrary"), ((rows, n), F32, 2), ((rows, n), BF16, 2),
                                ((LANES, LANES), BF16, 2)),
        name="prep%d" % n,
    )(w, _pair_split_matrix())


def _vec(bsz, *rows):
    rows = [jnp.broadcast_to(r, (bsz, D_MODEL)) for r in rows]
    rows += [jnp.zeros((bsz, D_MODEL), F32)] * (8 - len(rows))
    return jnp.stack(rows, axis=1)


def kernel(x, c, ada_w, ada_b, norm_g, ffn_w_gate, ffn_w_up, ffn_w_down, moba_w_qkv, moba_w_o,
           kv_ada_w, kv_ada_b, kv_norm_g, kv_w, dil_w_q, dil_w_o, final_g):
    bsz, seq, d = x.shape
    t = bsz * seq
    tables = _rope_tables(seq)
    natural_tables = tuple(tab[:1] for tab in tables)
    mod = _mod_call(c, ada_w, ada_b).reshape(DEPTH, bsz, 3, 3, d)
    kv_mod = _mod_call(c, kv_ada_w[None], kv_ada_b[None]).reshape(bsz, 2, d)
    head_expand = jnp.tile(jnp.repeat(jnp.eye(N_HEADS, dtype=BF16), HEAD_DIM, axis=1), (2, 1))
    q_scale = HEAD_DIM ** -0.5 * LOG2E

    ffn_w = (ffn_w_gate, ffn_w_up, ffn_w_down.astype(BF16))
    moba_wo, dil_wo = moba_w_o.astype(BF16), dil_w_o.astype(BF16)
    qkv_sections = ((True, q_scale, 1), (True, 1.0, 1), (False, 1.0, 1))
    q_sections = tuple((True, q_scale, dil) for dil in DILATIONS)
    kv_sections = tuple((is_k, 1.0, dil) for dil in DILATIONS for is_k in (True, False))
    moba_wqkv = _prep_call(moba_w_qkv, qkv_sections)
    dil_wq = _prep_call(dil_w_q, q_sections)
    kv_wp = _prep_call(kv_w[None], kv_sections)

    xf = x.reshape(t, d)
    kvs = None
    for layer in range(DEPTH):
        m = mod[layer]
        if layer == N_A_LAYERS:
            kvs = _proj_call(xf, _vec(bsz, kv_norm_g, kv_mod[:, 0], kv_mod[:, 1]), tables, kv_wp, 0,
                             sections=kv_sections, fused_out=False, bsz=bsz)
        xf = _ffn_call(xf, _vec(bsz, norm_g[layer, 0], m[:, 0, 0], m[:, 0, 1], m[:, 0, 2]),
                       ffn_w, layer, 0)

        mix_vec = _vec(bsz, norm_g[layer, 1], m[:, 1, 0], m[:, 1, 1])
        if layer < N_A_LAYERS:
            lb = layer
            qkv, = _proj_call(xf, mix_vec, natural_tables, moba_wqkv, lb,
                              sections=qkv_sections, fused_out=True, bsz=bsz)
            mixer = (_moba_call(qkv, bsz), moba_wo)
            mode = "moba"
        else:
            lb = layer - N_A_LAYERS
            qs = _proj_call(xf, mix_vec, tables, dil_wq, lb,
                            sections=q_sections, fused_out=False, bsz=bsz)
            outs, lses = [], []
            for g, (window, dil) in enumerate(DILATED_BRANCHES):
                assert window // dil == BAND
                o_g, lse_g = _band_call(qs[g].reshape(t, d), kvs[2 * g].reshape(t, d),
                                        kvs[2 * g + 1].reshape(t, d), dil, bsz)
                outs.append(o_g if dil == 1 else o_g.reshape(bsz, dil, seq // dil, d))
                lses.append(_lse_token_order(lse_g, dil))
            mixer = (*outs, *lses, head_expand, dil_wo)
            mode = "dil"
        xf = _ffn_call(xf, _vec(bsz, norm_g[layer, 2], m[:, 2, 0], m[:, 2, 1], m[:, 2, 2],
                                m[:, 1, 2], final_g),
                       ffn_w, layer, 1, mode=mode, final=(layer == DEPTH - 1), mixer=mixer, mixer_layer=lb)
    return xf.reshape(bsz, seq, d)
```
